```python
import jax, jax.numpy as jnp
from jax import lax
import numpy as np

D_MODEL = 1024
BATCH = 8
SEQ = 2048
DEPTH = 2
DEC_BATCH = 128
DEC_SEQ = 1
PAST_LEN = 16384
PAGE_SIZE = 128

E_A = D_MODEL
G_A = 4
HD_A = E_A // G_A
CHUNK = 128
E_B = D_MODEL
CONV_W = 3
D_FF = 2816
EPS = 1e-6
IN_COLS = 2 * E_A + 3 * E_B + 2 * D_MODEL
SPLITS = (E_A, 2 * E_A, 2 * E_A + E_B, 2 * E_A + 2 * E_B, 2 * E_A + 3 * E_B, 2 * E_A + 3 * E_B + D_MODEL)

kernel_name = "hybrid_chunkmlp_shortconv_macaron_step"


def rms_norm(x, g):
    xf = x.astype(jnp.float32)
    y = xf * lax.rsqrt(jnp.mean(xf * xf, axis=-1, keepdims=True) + EPS)
    return (y * g.astype(jnp.float32)).astype(x.dtype)


def layer_norm(x, g, b):
    xf = x.astype(jnp.float32)
    mu = jnp.mean(xf, axis=-1, keepdims=True)
    xc = xf - mu
    y = xc * lax.rsqrt(jnp.mean(xc * xc, axis=-1, keepdims=True) + EPS)
    return (y * g.astype(jnp.float32) + b.astype(jnp.float32)).astype(x.dtype)


def swiglu(x, w_gate, w_up, w_down):
    return (jax.nn.silu(x @ w_gate) * (x @ w_up)) @ w_down


def chunk_spatial_gate(u, v, w_s, b_s):
    bsz, L, _ = v.shape
    Lp = -(-L // CHUNK) * CHUNK
    vp = jnp.pad(v, ((0, 0), (0, Lp - L), (0, 0)))
    vc = vp.reshape(bsz, Lp // CHUNK, CHUNK, G_A, HD_A)
    causal = jnp.tril(jnp.ones((CHUNK, CHUNK), dtype=bool))
    w = jnp.where(causal[None], w_s, jnp.zeros((), w_s.dtype))
    z = jnp.einsum('gts,bcsgd->bctgd', w, vc) + jnp.transpose(b_s)[None, None, :, :, None]
    z = z.reshape(bsz, Lp, E_A)[:, :L]
    return u * z


def hybrid_layer(x, conv_buf, ffn1_norm, ffn1_w_gate, ffn1_w_up, ffn1_w_down, mix_norm, w_in, b_in,
                 v_ln_gain, v_ln_bias, w_spatial, b_spatial, conv_w, w_out,
                 ffn2_norm, ffn2_w_gate, ffn2_w_up, ffn2_w_down):
    h = x + 0.5 * swiglu(rms_norm(x, ffn1_norm), ffn1_w_gate, ffn1_w_up, ffn1_w_down)
    n = rms_norm(h, mix_norm)
    z = n @ w_in + b_in
    u, v, gate_b, gate_c, x_in, g_a, g_b = jnp.split(z, SPLITS, axis=-1)
    u = jax.nn.gelu(u, approximate=False)
    v = layer_norm(jax.nn.gelu(v, approximate=False), v_ln_gain, v_ln_bias)
    y_a = chunk_spatial_gate(u, v, w_spatial, b_spatial)
    L = x.shape[1]
    xg = gate_c * x_in
    xc = jnp.concatenate([conv_buf.astype(xg.dtype), xg], axis=1)
    conv = conv_w[0] * xc[:, 0:L]
    for k in range(1, CONV_W):
        conv = conv + conv_w[k] * xc[:, k:k + L]
    y_b = gate_b * conv
    new_buf = xc[:, -(CONV_W - 1):]
    m = jax.nn.sigmoid(g_a) * y_a + jax.nn.sigmoid(g_b) * y_b
    h = h + m @ w_out
    h = h + 0.5 * swiglu(rms_norm(h, ffn2_norm), ffn2_w_gate, ffn2_w_up, ffn2_w_down)
    return h, new_buf, v


def setup_inputs(seed: int = 0) -> dict:
    key = jax.random.key(seed)
    ks = jax.random.split(key, 24)
    f32 = jnp.float32
    nrm = lambda k, shape, s: jax.random.normal(k, shape, f32) * s
    d = D_MODEL
    return {
        "x_prompt": nrm(ks[0], (BATCH, SEQ, d), 1.0),
        "x_sample": nrm(ks[1], (DEC_BATCH, DEC_SEQ, d), 1.0),
        "state_conv": nrm(ks[2], (DEPTH, DEC_BATCH, CONV_W - 1, E_B), 0.5),
        "ffn1_norm": 1.0 + nrm(ks[3], (DEPTH, d), 0.02),
        "ffn1_w_gate": nrm(ks[4], (DEPTH, d, D_FF), d ** -0.5),
        "ffn1_w_up": nrm(ks[5], (DEPTH, d, D_FF), d ** -0.5),
        "ffn1_w_down": nrm(ks[6], (DEPTH, D_FF, d), D_FF ** -0.5),
        "mix_norm": 1.0 + nrm(ks[7], (DEPTH, d), 0.02),
        "w_in": nrm(ks[8], (DEPTH, d, IN_COLS), d ** -0.5),
        "b_in": nrm(ks[9], (DEPTH, IN_COLS), 0.02),
        "v_ln_gain": 1.0 + nrm(ks[10], (DEPTH, E_A), 0.02),
        "v_ln_bias": nrm(ks[11], (DEPTH, E_A), 0.02),
        "w_spatial": nrm(ks[12], (DEPTH, G_A, CHUNK, CHUNK), CHUNK ** -0.5),
        "b_spatial": 1.0 + nrm(ks[13], (DEPTH, G_A, CHUNK), 0.02),
        "conv_w": nrm(ks[14], (DEPTH, CONV_W, E_B), CONV_W ** -0.5),
        "w_out": nrm(ks[15], (DEPTH, d, d), d ** -0.5),
        "ffn2_norm": 1.0 + nrm(ks[16], (DEPTH, d), 0.02),
        "ffn2_w_gate": nrm(ks[17], (DEPTH, d, D_FF), d ** -0.5),
        "ffn2_w_up": nrm(ks[18], (DEPTH, d, D_FF), d ** -0.5),
        "ffn2_w_down": nrm(ks[19], (DEPTH, D_FF, d), D_FF ** -0.5),
        "final_norm": 1.0 + nrm(ks[20], (d,), 0.02),
    }


def reference(x_prompt, x_sample, state_conv, ffn1_norm, ffn1_w_gate, ffn1_w_up, ffn1_w_down,
              mix_norm, w_in, b_in, v_ln_gain, v_ln_bias, w_spatial, b_spatial, conv_w, w_out,
              ffn2_norm, ffn2_w_gate, ffn2_w_up, ffn2_w_down, final_norm):
    hp = x_prompt
    hs = x_sample
    conv_p_list, conv_s_list, v_s_list = [], [], []
    for l in range(DEPTH):
        params = (ffn1_norm[l], ffn1_w_gate[l], ffn1_w_up[l], ffn1_w_down[l], mix_norm[l], w_in[l], b_in[l],
                  v_ln_gain[l], v_ln_bias[l], w_spatial[l], b_spatial[l], conv_w[l], w_out[l],
                  ffn2_norm[l], ffn2_w_gate[l], ffn2_w_up[l], ffn2_w_down[l])
        zero_buf = jnp.zeros((hp.shape[0], CONV_W - 1, E_B), dtype=hp.dtype)
        hp, buf_p, _ = hybrid_layer(hp, zero_buf, *params)
        hs, buf_s, v_s = hybrid_layer(hs, state_conv[l], *params)
        conv_p_list.append(buf_p)
        conv_s_list.append(buf_s)
        v_s_list.append(v_s)
    y_prompt = rms_norm(hp, final_norm)
    y_sample = rms_norm(hs, final_norm)
    new_conv_prompt = jnp.stack(conv_p_list, axis=0)
    new_conv_sample = jnp.stack(conv_s_list, axis=0)
    new_chunk_v_sample = jnp.stack(v_s_list, axis=0)
    return (y_prompt, y_sample, new_conv_prompt, new_conv_sample, new_chunk_v_sample)
```

```python
import functools

import jax
import jax.numpy as jnp
import numpy as np
from jax import lax
from jax.experimental import pallas as pl
from jax.experimental.pallas import tpu as pltpu

D_MODEL = 1024
E_A = D_MODEL
G_A = 4
HD_A = E_A // G_A
CHUNK = 128
E_B = D_MODEL
CONV_W = 3
D_FF = 2816
EPS = 1e-6
SQRT_HALF = float(np.sqrt(0.5).astype(np.float32))
IN_COLS = 2 * E_A + 3 * E_B + 2 * D_MODEL
COL_U, COL_V, COL_B, COL_C, COL_X, COL_GA, COL_GB = (k * D_MODEL for k in range(7))

ROW_TILE = 512
FF_CHUNK = 256
COL_BLOCK = HD_A
HALO = 8
VMEM_LIMIT_BYTES = 56 * 1024 * 1024

F32 = jnp.float32
BF16 = jnp.bfloat16


def _rms_norm(x, g):
    return (x * lax.rsqrt(jnp.mean(x * x, axis=-1, keepdims=True) + EPS)) * g


def _layer_norm(x, g, b):
    mu = jnp.mean(x, axis=-1, keepdims=True)
    xc = x - mu
    return (xc * lax.rsqrt(jnp.mean(xc * xc, axis=-1, keepdims=True) + EPS)) * g + b


def _gelu(x):
    return 0.5 * x * (1.0 + lax.erf(x * SQRT_HALF))


def _dot(a, b):
    return jnp.dot(a, b, preferred_element_type=F32)


def _resident(shape):
    return pl.BlockSpec(shape, lambda i: (0,) * len(shape), pipeline_mode=pl.Buffered(1))


def _params():
    return pltpu.CompilerParams(dimension_semantics=("arbitrary",), vmem_limit_bytes=VMEM_LIMIT_BYTES)


def _ffn_kernel(x_ref, g_ref, wg_ref, wu_ref, wd_ref, fg_ref, o_ref, *, final):
    x = x_ref[...]
    xn = _rms_norm(x, g_ref[...]).astype(BF16)
    acc = None
    for c in range(D_FF // FF_CHUNK):
        sl = slice(c * FF_CHUNK, (c + 1) * FF_CHUNK)
        gate = _dot(xn, wg_ref[:, sl])
        up = _dot(xn, wu_ref[:, sl])
        a = (jax.nn.silu(gate) * up).astype(BF16)
        d = _dot(a, wd_ref[sl, :])
        acc = d if acc is None else acc + d
    h = x + 0.5 * acc
    if final:
        h = _rms_norm(h, fg_ref[...])
    o_ref[...] = h


def _ffn(x, norm_g, wg, wu, wd, final_g, *, tm, final):
    rows = x.shape[0]
    row_spec = pl.BlockSpec((tm, D_MODEL), lambda i: (i, 0))
    return pl.pallas_call(
        functools.partial(_ffn_kernel, final=final),
        grid=(rows // tm,),
        in_specs=[row_spec, _resident((1, D_MODEL)), _resident((D_MODEL, D_FF)), _resident((D_MODEL, D_FF)),
                  _resident((D_FF, D_MODEL)), _resident((1, D_MODEL))],
        out_specs=row_spec,
        out_shape=jax.ShapeDtypeStruct((rows, D_MODEL), F32),
        compiler_params=_params(),
        name="ffn_final" if final else "ffn",
    )(x, norm_g, wg, wu, wd, final_g)


def _causal_spatial(ws_ref, g):
    t = lax.broadcasted_iota(jnp.int32, (CHUNK, CHUNK), 0)
    s = lax.broadcasted_iota(jnp.int32, (CHUNK, CHUNK), 1)
    return jnp.where(s <= t, ws_ref[g], 0.0).astype(BF16)


def _mix_kernel(h_ref, ng_ref, win_ref, bin_ref, lng_ref, lnb_ref, ws_ref, bs_ref, cw_ref, wout_ref,
                o_ref, buf_ref, vn_s, xg_s, m_s, *, tm, tiles_per_seq):
    i = pl.program_id(0)
    h = h_ref[...]
    n = _rms_norm(h, ng_ref[...]).astype(BF16)

    def proj(col, width=COL_BLOCK):
        return _dot(n, win_ref[:, col:col + width]) + bin_ref[:, col:col + width]

    v = _layer_norm(_gelu(proj(COL_V, E_A)), lng_ref[...], lnb_ref[...])
    vn_s[...] = v.astype(BF16)

    @pl.when(i % tiles_per_seq == 0)
    def _():
        xg_s[0:HALO, :] = jnp.zeros((HALO, E_B), F32)

    for g in range(G_A):
        c0 = g * COL_BLOCK
        cs = slice(c0, c0 + COL_BLOCK)
        u = _gelu(proj(COL_U + c0))
        w_mix = _causal_spatial(ws_ref, g)
        z = jnp.concatenate(
            [_dot(w_mix, vn_s[c * CHUNK:(c + 1) * CHUNK, cs]) + bs_ref[g] for c in range(tm // CHUNK)], axis=0)
        y_a = u * z
        gate_b = proj(COL_B + c0)
        xg = proj(COL_C + c0) * proj(COL_X + c0)
        xg_s[HALO:HALO + tm, cs] = xg
        conv = cw_ref[0:1, cs] * xg_s[HALO - 2:HALO - 2 + tm, cs]
        conv = conv + cw_ref[1:2, cs] * xg_s[HALO - 1:HALO - 1 + tm, cs]
        conv = conv + cw_ref[2:3, cs] * xg
        y_b = gate_b * conv
        m = jax.nn.sigmoid(proj(COL_GA + c0)) * y_a + jax.nn.sigmoid(proj(COL_GB + c0)) * y_b
        m_s[:, cs] = m.astype(BF16)

    last = xg_s[HALO + tm - 2:HALO + tm, :]
    buf_ref[0] = last
    xg_s[HALO - 2:HALO, :] = last
    o_ref[...] = h + _dot(m_s[...], wout_ref[...])


def _mix(h, norm_g, w_in, b_in, ln_g, ln_b, w_s, b_s, conv_w, w_out, *, tm, seq):
    rows = h.shape[0]
    tiles_per_seq = seq // tm
    row_spec = pl.BlockSpec((tm, D_MODEL), lambda i: (i, 0))
    return pl.pallas_call(
        functools.partial(_mix_kernel, tm=tm, tiles_per_seq=tiles_per_seq),
        grid=(rows // tm,),
        in_specs=[row_spec, _resident((1, D_MODEL)), _resident((D_MODEL, IN_COLS)), _resident((1, IN_COLS)),
                  _resident((1, E_A)), _resident((1, E_A)), _resident((G_A, CHUNK, CHUNK)),
                  _resident((G_A, CHUNK, 1)), _resident((CONV_W, E_B)), _resident((D_MODEL, D_MODEL))],
        out_specs=[row_spec, pl.BlockSpec((1, CONV_W - 1, E_B), lambda i: (i // tiles_per_seq, 0, 0))],
        out_shape=[jax.ShapeDtypeStruct((rows, D_MODEL), F32),
                   jax.ShapeDtypeStruct((rows // seq, CONV_W - 1, E_B), F32)],
        scratch_shapes=[pltpu.VMEM((tm, E_A), BF16), pltpu.VMEM((HALO + tm, E_B), F32),
                        pltpu.VMEM((tm, D_MODEL), BF16)],
        compiler_params=_params(),
        name="mix",
    )(h, norm_g, w_in, b_in, ln_g, ln_b, w_s, b_s, conv_w, w_out)


def _mix1_kernel(h_ref, s0_ref, s1_ref, ng_ref, win_ref, bin_ref, lng_ref, lnb_ref, ws_ref, bs_ref, cw_ref,
                 wout_ref, o_ref, xg_ref, v_ref, m_s):
    h = h_ref[...]
    n = _rms_norm(h, ng_ref[...]).astype(BF16)

    def proj(col, width=COL_BLOCK):
        return _dot(n, win_ref[:, col:col + width]) + bin_ref[:, col:col + width]

    v = _layer_norm(_gelu(proj(COL_V, E_A)), lng_ref[...], lnb_ref[...])
    v_ref[...] = v
    for g in range(G_A):
        c0 = g * COL_BLOCK
        cs = slice(c0, c0 + COL_BLOCK)
        u = _gelu(proj(COL_U + c0))
        z = ws_ref[g][0:1, 0:1] * v[:, cs] + bs_ref[g][0:1, :]
        y_a = u * z
        gate_b = proj(COL_B + c0)
        xg = proj(COL_C + c0) * proj(COL_X + c0)
        xg_ref[:, cs] = xg
        conv = cw_ref[0:1, cs] * s0_ref[:, cs]
        conv = conv + cw_ref[1:2, cs] * s1_ref[:, cs]
        conv = conv + cw_ref[2:3, cs] * xg
        y_b = gate_b * conv
        m = jax.nn.sigmoid(proj(COL_GA + c0)) * y_a + jax.nn.sigmoid(proj(COL_GB + c0)) * y_b
        m_s[:, cs] = m.astype(BF16)
    o_ref[...] = h + _dot(m_s[...], wout_ref[...])


def _mix1(h, s0, s1, norm_g, w_in, b_in, ln_g, ln_b, w_s, b_s, conv_w, w_out):
    rows = h.shape[0]
    full = _resident((rows, D_MODEL))
    return pl.pallas_call(
        _mix1_kernel,
        grid=(1,),
        in_specs=[full, full, full, _resident((1, D_MODEL)), _resident((D_MODEL, IN_COLS)),
                  _resident((1, IN_COLS)), _resident((1, E_A)), _resident((1, E_A)),
                  _resident((G_A, CHUNK, CHUNK)), _resident((G_A, CHUNK, 1)), _resident((CONV_W, E_B)),
                  _resident((D_MODEL, D_MODEL))],
        out_specs=[pl.BlockSpec((rows, D_MODEL), lambda i: (0, 0))] * 3,
        out_shape=[jax.ShapeDtypeStruct((rows, D_MODEL), F32)] * 3,
        scratch_shapes=[pltpu.VMEM((rows, D_MODEL), BF16)],
        compiler_params=_params(),
        name="mix1",
    )(h, s0, s1, norm_g, w_in, b_in, ln_g, ln_b, w_s, b_s, conv_w, w_out)


def kernel(x_prompt, x_sample, state_conv, ffn1_norm, ffn1_w_gate, ffn1_w_up, ffn1_w_down, mix_norm, w_in, b_in,
           v_ln_gain, v_ln_bias, w_spatial, b_spatial, conv_w, w_out, ffn2_norm, ffn2_w_gate, ffn2_w_up,
           ffn2_w_down, final_norm):
    batch, seq, _ = x_prompt.shape
    dec_batch, dec_seq, _ = x_sample.shape
    depth = w_in.shape[0]
    assert dec_seq == 1 and seq % ROW_TILE == 0 and ROW_TILE % CHUNK == 0

    hp = x_prompt.reshape(batch * seq, D_MODEL)
    hs = x_sample.reshape(dec_batch, D_MODEL)
    fin_g = final_norm.reshape(1, D_MODEL)
    row = lambda a: a.reshape(1, -1)
    conv_p, conv_s, v_s = [], [], []
    for l in range(depth):
        last = l == depth - 1
        ffn1 = (row(ffn1_norm[l]), ffn1_w_gate[l].astype(BF16), ffn1_w_up[l].astype(BF16),
                ffn1_w_down[l].astype(BF16), fin_g)
        ffn2 = (row(ffn2_norm[l]), ffn2_w_gate[l].astype(BF16), ffn2_w_up[l].astype(BF16),
                ffn2_w_down[l].astype(BF16), fin_g)
        mix = (row(mix_norm[l]), w_in[l].astype(BF16), row(b_in[l]), row(v_ln_gain[l]), row(v_ln_bias[l]),
               w_spatial[l], b_spatial[l][:, :, None], conv_w[l], w_out[l].astype(BF16))

        hp = _ffn(hp, *ffn1, tm=ROW_TILE, final=False)
        hs = _ffn(hs, *ffn1, tm=dec_batch, final=False)
        hp, buf_p = _mix(hp, *mix, tm=ROW_TILE, seq=seq)
        hs, xg_s, vn_s = _mix1(hs, state_conv[l, :, 0], state_conv[l, :, 1], *mix)
        hp = _ffn(hp, *ffn2, tm=ROW_TILE, final=last)
        hs = _ffn(hs, *ffn2, tm=dec_batch, final=last)

        conv_p.append(buf_p)
        conv_s.append(jnp.stack([state_conv[l, :, 1], xg_s], axis=1))
        v_s.append(vn_s.reshape(dec_batch, dec_seq, E_A))

    return (hp.reshape(batch, seq, D_MODEL), hs.reshape(dec_batch, dec_seq, D_MODEL),
            jnp.stack(conv_p, axis=0), jnp.stack(conv_s, axis=0), jnp.stack(v_s, axis=0))
```

```python
import functools

import jax
import jax.numpy as jnp
import numpy as np
from jax import lax
from jax.experimental import pallas as pl
from jax.experimental.pallas import tpu as pltpu

D_MODEL = 1024
E_A = D_MODEL
G_A = 4
HD_A = E_A // G_A
CHUNK = 128
E_B = D_MODEL
CONV_W = 3
D_FF = 2816
EPS = 1e-6
SQRT_HALF = float(np.sqrt(0.5).astype(np.float32))
IN_COLS = 2 * E_A + 3 * E_B + 2 * D_MODEL
COL_U, COL_V, COL_B, COL_C, COL_X, COL_GA, COL_GB = (k * D_MODEL for k in range(7))

ROW_TILE = 512
FF_CHUNK = 256
COL_BLOCK = HD_A
HALO = 8
VMEM_LIMIT_BYTES = 56 * 1024 * 1024

F32 = jnp.float32
BF16 = jnp.bfloat16


def _rms_norm(x, g):
    return (x * lax.rsqrt(jnp.mean(x * x, axis=-1, keepdims=True) + EPS)) * g


def _layer_norm(x, g, b):
    mu = jnp.mean(x, axis=-1, keepdims=True)
    xc = x - mu
    return (xc * lax.rsqrt(jnp.mean(xc * xc, axis=-1, keepdims=True) + EPS)) * g + b


def _gelu(x):
    return 0.5 * x * (1.0 + lax.erf(x * SQRT_HALF))


def _dot(a, b):
    return jnp.dot(a, b, preferred_element_type=F32)


def _resident(shape):
    return pl.BlockSpec(shape, lambda i: (0,) * len(shape), pipeline_mode=pl.Buffered(1))


def _params():
    return pltpu.CompilerParams(dimension_semantics=("arbitrary",), vmem_limit_bytes=VMEM_LIMIT_BYTES)


def _ffn_kernel(x_ref, g_ref, wg_ref, wu_ref, wd_ref, fg_ref, o_ref, *, final):
    x = x_ref[...]
    xn = _rms_norm(x, g_ref[...]).astype(BF16)
    acc = None
    for c in range(D_FF // FF_CHUNK):
        sl = slice(c * FF_CHUNK, (c + 1) * FF_CHUNK)
        gate = _dot(xn, wg_ref[:, sl])
        up = _dot(xn, wu_ref[:, sl])
        a = (jax.nn.silu(gate) * up).astype(BF16)
        d = _dot(a, wd_ref[sl, :])
        acc = d if acc is None else acc + d
    h = x + 0.5 * acc
    if final:
        h = _rms_norm(h, fg_ref[...])
    o_ref[...] = h


def _ffn(x, norm_g, wg, wu, wd, final_g, *, tm, final):
    rows = x.shape[0]
    row_spec = pl.BlockSpec((tm, D_MODEL), lambda i: (i, 0))
    return pl.pallas_call(
        functools.partial(_ffn_kernel, final=final),
        grid=(rows // tm,),
        in_specs=[row_spec, _resident((1, D_MODEL)), _resident((D_MODEL, D_FF)), _resident((D_MODEL, D_FF)),
                  _resident((D_FF, D_MODEL)), _resident((1, D_MODEL))],
        out_specs=row_spec,
        out_shape=jax.ShapeDtypeStruct((rows, D_MODEL), F32),
        compiler_params=_params(),
        name="ffn_final" if final else "ffn",
    )(x, norm_g, wg, wu, wd, final_g)


def _causal_spatial(ws_ref, g):
    t = lax.broadcasted_iota(jnp.int32, (CHUNK, CHUNK), 0)
    s = lax.broadcasted_iota(jnp.int32, (CHUNK, CHUNK), 1)
    return jnp.where(s <= t, ws_ref[g], 0.0).astype(BF16)


def _mix_kernel(h_ref, ng_ref, win_ref, bin_ref, lng_ref, lnb_ref, ws_ref, bs_ref, cw_ref, wout_ref,
                o_ref, buf_ref, vn_s, xg_s, yb_s, m_s, *, tm, tiles_per_seq):
    @pl.when(pl.program_id(0) % tiles_per_seq == 0)
    def _():
        xg_s[0:HALO, :] = jnp.zeros((HALO, E_B), F32)

    h = h_ref[...]
    n = _rms_norm(h, ng_ref[...]).astype(BF16)

    def proj(col, width=COL_BLOCK):
        return _dot(n, win_ref[:, col:col + width]) + bin_ref[:, col:col + width]

    vs = []
    for g in range(G_A):
        c0 = g * COL_BLOCK
        cs = slice(c0, c0 + COL_BLOCK)
        vs.append(_gelu(proj(COL_V + c0)))
        gate_b = proj(COL_B + c0)
        xg = proj(COL_C + c0) * proj(COL_X + c0)
        xg_s[HALO:HALO + tm, cs] = xg
        conv = cw_ref[0:1, cs] * xg_s[HALO - 2:HALO - 2 + tm, cs]
        conv = conv + cw_ref[1:2, cs] * xg_s[HALO - 1:HALO - 1 + tm, cs]
        conv = conv + cw_ref[2:3, cs] * xg
        y_b = gate_b * conv
        yb_s[:, cs] = jax.nn.sigmoid(proj(COL_GB + c0)) * y_b

    mu = sum(jnp.sum(v, axis=-1, keepdims=True) for v in vs) / E_A
    vs = [v - mu for v in vs]
    var = sum(jnp.sum(v * v, axis=-1, keepdims=True) for v in vs) / E_A
    rstd = lax.rsqrt(var + EPS)
    for g in range(G_A):
        cs = slice(g * COL_BLOCK, (g + 1) * COL_BLOCK)
        vn_s[:, cs] = ((vs[g] * rstd) * lng_ref[:, cs] + lnb_ref[:, cs]).astype(BF16)

    for g in range(G_A):
        c0 = g * COL_BLOCK
        cs = slice(c0, c0 + COL_BLOCK)
        u = _gelu(proj(COL_U + c0))
        w_mix = _causal_spatial(ws_ref, g)
        z = jnp.concatenate(
            [_dot(w_mix, vn_s[c * CHUNK:(c + 1) * CHUNK, cs]) + bs_ref[g] for c in range(tm // CHUNK)], axis=0)
        y_a = u * z
        m = jax.nn.sigmoid(proj(COL_GA + c0)) * y_a + yb_s[:, cs]
        m_s[:, cs] = m.astype(BF16)

    last = xg_s[HALO + tm - 2:HALO + tm, :]
    buf_ref[0] = last
    xg_s[HALO - 2:HALO, :] = last
    o_ref[...] = h + _dot(m_s[...], wout_ref[...])


def _mix(h, norm_g, w_in, b_in, ln_g, ln_b, w_s, b_s, conv_w, w_out, *, tm, seq):
    rows = h.shape[0]
    tiles_per_seq = seq // tm
    row_spec = pl.BlockSpec((tm, D_MODEL), lambda i: (i, 0))
    return pl.pallas_call(
        functools.partial(_mix_kernel, tm=tm, tiles_per_seq=tiles_per_seq),
        grid=(rows // tm,),
        in_specs=[row_spec, _resident((1, D_MODEL)), _resident((D_MODEL, IN_COLS)), _resident((1, IN_COLS)),
                  _resident((1, E_A)), _resident((1, E_A)), _resident((G_A, CHUNK, CHUNK)),
                  _resident((G_A, CHUNK, 1)), _resident((CONV_W, E_B)), _resident((D_MODEL, D_MODEL))],
        out_specs=[row_spec, pl.BlockSpec((1, CONV_W - 1, E_B), lambda i: (i // tiles_per_seq, 0, 0))],
        out_shape=[jax.ShapeDtypeStruct((rows, D_MODEL), F32),
                   jax.ShapeDtypeStruct((rows // seq, CONV_W - 1, E_B), F32)],
        scratch_shapes=[pltpu.VMEM((tm, E_A), BF16), pltpu.VMEM((HALO + tm, E_B), F32),
                        pltpu.VMEM((tm, E_B), F32), pltpu.VMEM((tm, D_MODEL), BF16)],
        compiler_params=_params(),
        name="mix",
    )(h, norm_g, w_in, b_in, ln_g, ln_b, w_s, b_s, conv_w, w_out)


def _mix1_kernel(h_ref, s0_ref, s1_ref, ng_ref, win_ref, bin_ref, lng_ref, lnb_ref, ws_ref, bs_ref, cw_ref,
                 wout_ref, o_ref, xg_ref, v_ref, m_s):
    h = h_ref[...]
    n = _rms_norm(h, ng_ref[...]).astype(BF16)

    def proj(col, width=COL_BLOCK):
        return _dot(n, win_ref[:, col:col + width]) + bin_ref[:, col:col + width]

    v = _layer_norm(_gelu(proj(COL_V, E_A)), lng_ref[...], lnb_ref[...])
    v_ref[...] = v
    for g in range(G_A):
        c0 = g * COL_BLOCK
        cs = slice(c0, c0 + COL_BLOCK)
        u = _gelu(proj(COL_U + c0))
        z = ws_ref[g][0:1, 0:1] * v[:, cs] + bs_ref[g][0:1, :]
        y_a = u * z
        gate_b = proj(COL_B + c0)
        xg = proj(COL_C + c0) * proj(COL_X + c0)
        xg_ref[:, cs] = xg
        conv = cw_ref[0:1, cs] * s0_ref[:, cs]
        conv = conv + cw_ref[1:2, cs] * s1_ref[:, cs]
        conv = conv + cw_ref[2:3, cs] * xg
        y_b = gate_b * conv
        m = jax.nn.sigmoid(proj(COL_GA + c0)) * y_a + jax.nn.sigmoid(proj(COL_GB + c0)) * y_b
        m_s[:, cs] = m.astype(BF16)
    o_ref[...] = h + _dot(m_s[...], wout_ref[...])


def _mix1(h, s0, s1, norm_g, w_in, b_in, ln_g, ln_b, w_s, b_s, conv_w, w_out):
    rows = h.shape[0]
    full = _resident((rows, D_MODEL))
    return pl.pallas_call(
        _mix1_kernel,
        grid=(1,),
        in_specs=[full, full, full, _resident((1, D_MODEL)), _resident((D_MODEL, IN_COLS)),
                  _resident((1, IN_COLS)), _resident((1, E_A)), _resident((1, E_A)),
                  _resident((G_A, CHUNK, CHUNK)), _resident((G_A, CHUNK, 1)), _resident((CONV_W, E_B)),
                  _resident((D_MODEL, D_MODEL))],
        out_specs=[pl.BlockSpec((rows, D_MODEL), lambda i: (0, 0))] * 3,
        out_shape=[jax.ShapeDtypeStruct((rows, D_MODEL), F32)] * 3,
        scratch_shapes=[pltpu.VMEM((rows, D_MODEL), BF16)],
        compiler_params=_params(),
        name="mix1",
    )(h, s0, s1, norm_g, w_in, b_in, ln_g, ln_b, w_s, b_s, conv_w, w_out)


def kernel(x_prompt, x_sample, state_conv, ffn1_norm, ffn1_w_gate, ffn1_w_up, ffn1_w_down, mix_norm, w_in, b_in,
           v_ln_gain, v_ln_bias, w_spatial, b_spatial, conv_w, w_out, ffn2_norm, ffn2_w_gate, ffn2_w_up,
           ffn2_w_down, final_norm):
    batch, seq, _ = x_prompt.shape
    dec_batch, dec_seq, _ = x_sample.shape
    depth = w_in.shape[0]
    assert dec_seq == 1 and seq % ROW_TILE == 0 and ROW_TILE % CHUNK == 0

    hp = x_prompt.reshape(batch * seq, D_MODEL)
    hs = x_sample.reshape(dec_batch, D_MODEL)
    fin_g = final_norm.reshape(1, D_MODEL)
    row = lambda a: a.reshape(1, -1)
    conv_p, conv_s, v_s = [], [], []
    for l in range(depth):
        last = l == depth - 1
        ffn1 = (row(ffn1_norm[l]), ffn1_w_gate[l].astype(BF16), ffn1_w_up[l].astype(BF16),
                ffn1_w_down[l].astype(BF16), fin_g)
        ffn2 = (row(ffn2_norm[l]), ffn2_w_gate[l].astype(BF16), ffn2_w_up[l].astype(BF16),
                ffn2_w_down[l].astype(BF16), fin_g)
        mix = (row(mix_norm[l]), w_in[l].astype(BF16), row(b_in[l]), row(v_ln_gain[l]), row(v_ln_bias[l]),
               w_spatial[l], b_spatial[l][:, :, None], conv_w[l], w_out[l].astype(BF16))

        hp = _ffn(hp, *ffn1, tm=ROW_TILE, final=False)
        hs = _ffn(hs, *ffn1, tm=dec_batch, final=False)
        hp, buf_p = _mix(hp, *mix, tm=ROW_TILE, seq=seq)
        hs, xg_s, vn_s = _mix1(hs, state_conv[l, :, 0], state_conv[l, :, 1], *mix)
        hp = _ffn(hp, *ffn2, tm=ROW_TILE, final=last)
        hs = _ffn(hs, *ffn2, tm=dec_batch, final=last)

        conv_p.append(buf_p)
        conv_s.append(jnp.stack([state_conv[l, :, 1], xg_s], axis=1))
        v_s.append(vn_s.reshape(dec_batch, dec_seq, E_A))

    return (hp.reshape(batch, seq, D_MODEL), hs.reshape(dec_batch, dec_seq, D_MODEL),
            jnp.stack(conv_p, axis=0), jnp.stack(conv_s, axis=0), jnp.stack(v_s, axis=0))
```

```python
import functools

import jax
import jax.numpy as jnp
import numpy as np
from jax import lax
from jax.experimental import pallas as pl
from jax.experimental.pallas import tpu as pltpu

D_MODEL = 1024
E_A = D_MODEL
G_A = 4
HD_A = E_A // G_A
CHUNK = 128
E_B = D_MODEL
CONV_W = 3
D_FF = 2816
EPS = 1e-6
SQRT_HALF = float(np.sqrt(0.5).astype(np.float32))
IN_COLS = 2 * E_A + 3 * E_B + 2 * D_MODEL
COL_U, COL_V, COL_B, COL_C, COL_X, COL_GA, COL_GB = (k * D_MODEL for k in range(7))

ROW_TILE = 512
FF_CHUNK = 256
COL_BLOCK = HD_A
HALO = 8
ROWS_W_FF_IN = 128
ROWS_W_FF_OUT = 352
ROWS_W_IN = 64
ROWS_W_OUT = 256
VMEM_LIMIT_BYTES = 56 * 1024 * 1024

F32 = jnp.float32
BF16 = jnp.bfloat16


def _rms_norm(x, g):
    return (x * lax.rsqrt(jnp.mean(x * x, axis=-1, keepdims=True) + EPS)) * g


def _layer_norm(x, g, b):
    mu = jnp.mean(x, axis=-1, keepdims=True)
    xc = x - mu
    return (xc * lax.rsqrt(jnp.mean(xc * xc, axis=-1, keepdims=True) + EPS)) * g + b


def _gelu(x):
    return 0.5 * x * (1.0 + lax.erf(x * SQRT_HALF))


def _dot(a, b):
    return jnp.dot(a, b, preferred_element_type=F32)


def _resident(shape):
    return pl.BlockSpec(shape, lambda i: (0,) * len(shape), pipeline_mode=pl.Buffered(1))


def _whole_out(shape):
    return pl.BlockSpec(shape, lambda i: (0,) * len(shape))


_HBM = pl.BlockSpec(memory_space=pl.ANY)


def _params():
    return pltpu.CompilerParams(dimension_semantics=("arbitrary",), vmem_limit_bytes=VMEM_LIMIT_BYTES)


def _stream_cast(src, dst, stage, sem):
    rows = stage.shape[1]
    n = src.shape[0] // rows
    assert n * rows == src.shape[0]

    def copy(k, slot):
        return pltpu.make_async_copy(src.at[pl.ds(k * rows, rows), :], stage.at[slot], sem.at[slot])

    copy(0, 0).start()

    def body(k, carry):
        slot = lax.rem(k, 2)

        @pl.when(k + 1 < n)
        def _():
            copy(k + 1, 1 - slot).start()

        copy(k, slot).wait()
        dst[pl.ds(pl.multiple_of(k * rows, rows), rows), :] = stage[slot].astype(BF16)
        return carry

    lax.fori_loop(0, n, body, 0)


def _ffn_kernel(x_ref, xs_ref, g_ref, wg_hbm, wu_hbm, wd_hbm, fg_ref, o_ref, os_ref,
                wg_s, wu_s, wd_s, stage_in, stage_out, sem, *, layer, final):
    step = pl.program_id(0)

    @pl.when(step == 0)
    def _():
        _stream_cast(wg_hbm.at[layer], wg_s, stage_in, sem)
        _stream_cast(wu_hbm.at[layer], wu_s, stage_in, sem)
        _stream_cast(wd_hbm.at[layer], wd_s, stage_out, sem)

    def half_step(x):
        xn = _rms_norm(x, g_ref[...]).astype(BF16)
        acc = None
        for c in range(D_FF // FF_CHUNK):
            sl = slice(c * FF_CHUNK, (c + 1) * FF_CHUNK)
            gate = _dot(xn, wg_s[:, sl])
            up = _dot(xn, wu_s[:, sl])
            a = (jax.nn.silu(gate) * up).astype(BF16)
            d = _dot(a, wd_s[sl, :])
            acc = d if acc is None else acc + d
        h = x + 0.5 * acc
        if final:
            h = _rms_norm(h, fg_ref[...])
        return h

    o_ref[...] = half_step(x_ref[...])

    @pl.when(step == pl.num_programs(0) - 1)
    def _():
        os_ref[...] = half_step(xs_ref[...])


def _ffn(x, xs, norm_g, wg, wu, wd, final_g, *, layer, final):
    rows, srows = x.shape[0], xs.shape[0]
    row_spec = pl.BlockSpec((ROW_TILE, D_MODEL), lambda i: (i, 0))
    return pl.pallas_call(
        functools.partial(_ffn_kernel, layer=layer, final=final),
        grid=(rows // ROW_TILE,),
        in_specs=[row_spec, _resident((srows, D_MODEL)), _resident((1, D_MODEL)), _HBM, _HBM, _HBM,
                  _resident((1, D_MODEL))],
        out_specs=[row_spec, _whole_out((srows, D_MODEL))],
        out_shape=[jax.ShapeDtypeStruct((rows, D_MODEL), F32), jax.ShapeDtypeStruct((srows, D_MODEL), F32)],
        scratch_shapes=[pltpu.VMEM((D_MODEL, D_FF), BF16), pltpu.VMEM((D_MODEL, D_FF), BF16),
                        pltpu.VMEM((D_FF, D_MODEL), BF16),
                        pltpu.VMEM((2, ROWS_W_FF_IN, D_FF), F32), pltpu.VMEM((2, ROWS_W_FF_OUT, D_MODEL), F32),
                        pltpu.SemaphoreType.DMA((2,))],
        compiler_params=_params(),
        name="ffn_final" if final else "ffn",
    )(x, xs, norm_g, wg, wu, wd, final_g)


def _causal_spatial(ws_ref, g):
    t = lax.broadcasted_iota(jnp.int32, (CHUNK, CHUNK), 0)
    s = lax.broadcasted_iota(jnp.int32, (CHUNK, CHUNK), 1)
    return jnp.where(s <= t, ws_ref[g], 0.0).astype(BF16)


def _mix_kernel(h_ref, hs_ref, s0_ref, s1_ref, ng_ref, win_hbm, bin_ref, lng_ref, lnb_ref, ws_ref, bs_ref, cw_ref,
                wout_hbm, o_ref, buf_ref, os_ref, xgs_ref, vs_ref,
                win_s, wout_s, stage_in, stage_out, sem, vn_s, xg_s, yb_s, m_s, ms_s, *, layer, tm, tiles_per_seq):
    step = pl.program_id(0)

    @pl.when(step == 0)
    def _():
        _stream_cast(win_hbm.at[layer], win_s, stage_in, sem)
        _stream_cast(wout_hbm.at[layer], wout_s, stage_out, sem)

    @pl.when(step % tiles_per_seq == 0)
    def _():
        xg_s[0:HALO, :] = jnp.zeros((HALO, E_B), F32)

    h = h_ref[...]
    n = _rms_norm(h, ng_ref[...]).astype(BF16)

    def proj(col, width=COL_BLOCK):
        return _dot(n, win_s[:, col:col + width]) + bin_ref[:, col:col + width]

    vs = []
    for g in range(G_A):
        c0 = g * COL_BLOCK
        cs = slice(c0, c0 + COL_BLOCK)
        vs.append(_gelu(proj(COL_V + c0)))
        gate_b = proj(COL_B + c0)
        xg = proj(COL_C + c0) * proj(COL_X + c0)
        xg_s[HALO:HALO + tm, cs] = xg
        conv = cw_ref[0:1, cs] * xg_s[HALO - 2:HALO - 2 + tm, cs]
        conv = conv + cw_ref[1:2, cs] * xg_s[HALO - 1:HALO - 1 + tm, cs]
        conv = conv + cw_ref[2:3, cs] * xg
        y_b = gate_b * conv
        yb_s[:, cs] = jax.nn.sigmoid(proj(COL_GB + c0)) * y_b

    mu = sum(jnp.sum(v, axis=-1, keepdims=True) for v in vs) / E_A
    vs = [v - mu for v in vs]
    var = sum(jnp.sum(v * v, axis=-1, keepdims=True) for v in vs) / E_A
    rstd = lax.rsqrt(var + EPS)
    for g in range(G_A):
        cs = slice(g * COL_BLOCK, (g + 1) * COL_BLOCK)
        vn_s[:, cs] = ((vs[g] * rstd) * lng_ref[:, cs] + lnb_ref[:, cs]).astype(BF16)

    for g in range(G_A):
        c0 = g * COL_BLOCK
        cs = slice(c0, c0 + COL_BLOCK)
        u = _gelu(proj(COL_U + c0))
        w_mix = _causal_spatial(ws_ref, g)
        z = jnp.concatenate(
            [_dot(w_mix, vn_s[c * CHUNK:(c + 1) * CHUNK, cs]) + bs_ref[g] for c in range(tm // CHUNK)], axis=0)
        y_a = u * z
        m = jax.nn.sigmoid(proj(COL_GA + c0)) * y_a + yb_s[:, cs]
        m_s[:, cs] = m.astype(BF16)

    last = xg_s[HALO + tm - 2:HALO + tm, :]
    buf_ref[0] = last
    xg_s[HALO - 2:HALO, :] = last
    o_ref[...] = h + _dot(m_s[...], wout_s[...])

    @pl.when(step == pl.num_programs(0) - 1)
    def _():
        hs = hs_ref[...]
        ns = _rms_norm(hs, ng_ref[...]).astype(BF16)

        def proj_s(col, width=COL_BLOCK):
            return _dot(ns, win_s[:, col:col + width]) + bin_ref[:, col:col + width]

        v = _layer_norm(_gelu(proj_s(COL_V, E_A)), lng_ref[...], lnb_ref[...])
        vs_ref[...] = v
        for g in range(G_A):
            c0 = g * COL_BLOCK
            cs = slice(c0, c0 + COL_BLOCK)
            u = _gelu(proj_s(COL_U + c0))
            z = ws_ref[g][0:1, 0:1] * v[:, cs] + bs_ref[g][0:1, :]
            y_a = u * z
            gate_b = proj_s(COL_B + c0)
            xg = proj_s(COL_C + c0) * proj_s(COL_X + c0)
            xgs_ref[:, cs] = xg
            conv = cw_ref[0:1, cs] * s0_ref[:, cs]
            conv = conv + cw_ref[1:2, cs] * s1_ref[:, cs]
            conv = conv + cw_ref[2:3, cs] * xg
            y_b = gate_b * conv
            m = jax.nn.sigmoid(proj_s(COL_GA + c0)) * y_a + jax.nn.sigmoid(proj_s(COL_GB + c0)) * y_b
            ms_s[:, cs] = m.astype(BF16)
        os_ref[...] = hs + _dot(ms_s[...], wout_s[...])


def _mix(h, hs, s0, s1, norm_g, w_in, b_in, ln_g, ln_b, w_s, b_s, conv_w, w_out, *, layer, seq):
    rows, srows = h.shape[0], hs.shape[0]
    tm = ROW_TILE
    tiles_per_seq = seq // tm
    row_spec = pl.BlockSpec((tm, D_MODEL), lambda i: (i, 0))
    sample = _resident((srows, D_MODEL))
    sample_out = _whole_out((srows, D_MODEL))
    return pl.pallas_call(
        functools.partial(_mix_kernel, layer=layer, tm=tm, tiles_per_seq=tiles_per_seq),
        grid=(rows // tm,),
        in_specs=[row_spec, sample, sample, sample, _resident((1, D_MODEL)), _HBM, _resident((1, IN_COLS)),
                  _resident((1, E_A)), _resident((1, E_A)), _resident((G_A, CHUNK, CHUNK)),
                  _resident((G_A, CHUNK, 1)), _resident((CONV_W, E_B)), _HBM],
        out_specs=[row_spec, pl.BlockSpec((1, CONV_W - 1, E_B), lambda i: (i // tiles_per_seq, 0, 0)),
                   sample_out, sample_out, sample_out],
        out_shape=[jax.ShapeDtypeStruct((rows, D_MODEL), F32),
                   jax.ShapeDtypeStruct((rows // seq, CONV_W - 1, E_B), F32)]
                  + [jax.ShapeDtypeStruct((srows, D_MODEL), F32)] * 3,
        scratch_shapes=[pltpu.VMEM((D_MODEL, IN_COLS), BF16), pltpu.VMEM((D_MODEL, D_MODEL), BF16),
                        pltpu.VMEM((2, ROWS_W_IN, IN_COLS), F32), pltpu.VMEM((2, ROWS_W_OUT, D_MODEL), F32),
                        pltpu.SemaphoreType.DMA((2,)),
                        pltpu.VMEM((tm, E_A), BF16), pltpu.VMEM((HALO + tm, E_B), F32),
                        pltpu.VMEM((tm, E_B), F32), pltpu.VMEM((tm, D_MODEL), BF16),
                        pltpu.VMEM((srows, D_MODEL), BF16)],
        compiler_params=_params(),
        name="mix",
    )(h, hs, s0, s1, norm_g, w_in, b_in, ln_g, ln_b, w_s, b_s, conv_w, w_out)


def kernel(x_prompt, x_sample, state_conv, ffn1_norm, ffn1_w_gate, ffn1_w_up, ffn1_w_down, mix_norm, w_in, b_in,
           v_ln_gain, v_ln_bias, w_spatial, b_spatial, conv_w, w_out, ffn2_norm, ffn2_w_gate, ffn2_w_up,
           ffn2_w_down, final_norm):
    batch, seq, _ = x_prompt.shape
    dec_batch, dec_seq, _ = x_sample.shape
    depth = w_in.shape[0]
    assert dec_seq == 1 and seq % ROW_TILE == 0 and ROW_TILE % CHUNK == 0

    hp = x_prompt.reshape(batch * seq, D_MODEL)
    hs = x_sample.reshape(dec_batch, D_MODEL)
    fin_g = final_norm.reshape(1, D_MODEL)
    row = lambda a: a.reshape(1, -1)
    conv_p, conv_s, v_s = [], [], []
    for l in range(depth):
        last = l == depth - 1
        hp, hs = _ffn(hp, hs, row(ffn1_norm[l]), ffn1_w_gate, ffn1_w_up, ffn1_w_down, fin_g, layer=l, final=False)
        hp, buf_p, hs, xg_s, vn_s = _mix(
            hp, hs, state_conv[l, :, 0], state_conv[l, :, 1], row(mix_norm[l]), w_in, row(b_in[l]),
            row(v_ln_gain[l]), row(v_ln_bias[l]), w_spatial[l], b_spatial[l][:, :, None], conv_w[l], w_out,
            layer=l, seq=seq)
        hp, hs = _ffn(hp, hs, row(ffn2_norm[l]), ffn2_w_gate, ffn2_w_up, ffn2_w_down, fin_g, layer=l, final=last)

        conv_p.append(buf_p)
        conv_s.append(jnp.stack([state_conv[l, :, 1], xg_s], axis=1))
        v_s.append(vn_s.reshape(dec_batch, dec_seq, E_A))

    return (hp.reshape(batch, seq, D_MODEL), hs.reshape(dec_batch, dec_seq, D_MODEL),
            jnp.stack(conv_p, axis=0), jnp.stack(conv_s, axis=0), jnp.stack(v_s, axis=0))
```

```python
import functools

import jax
import jax.numpy as jnp
import numpy as np
from jax import lax
from jax.experimental import pallas as pl
from jax.experimental.pallas import tpu as pltpu

D_MODEL = 1024
E_A = D_MODEL
G_A = 4
HD_A = E_A // G_A
CHUNK = 128
E_B = D_MODEL
CONV_W = 3
D_FF = 2816
EPS = 1e-6
SQRT_HALF = float(np.sqrt(0.5).astype(np.float32))
IN_COLS = 2 * E_A + 3 * E_B + 2 * D_MODEL
COL_U, COL_V, COL_B, COL_C, COL_X, COL_GA, COL_GB = (k * D_MODEL for k in range(7))

ROW_TILE = 512
FF_CHUNK = 256
COL_BLOCK = HD_A
HALO = 8
FFN_SLOTS = 2
MIX_SLOTS = 2
VMEM_LIMIT_BYTES = 56 * 1024 * 1024

F32 = jnp.float32
BF16 = jnp.bfloat16


def _rms_norm(x, g):
    return (x * lax.rsqrt(jnp.mean(x * x, axis=-1, keepdims=True) + EPS)) * g


def _layer_norm(x, g, b):
    mu = jnp.mean(x, axis=-1, keepdims=True)
    xc = x - mu
    return (xc * lax.rsqrt(jnp.mean(xc * xc, axis=-1, keepdims=True) + EPS)) * g + b


def _gelu(x):
    return 0.5 * x * (1.0 + lax.erf(x * SQRT_HALF))


def _dot(a, b):
    return jnp.dot(a, b, preferred_element_type=F32)


def _resident(shape):
    return pl.BlockSpec(shape, lambda i: (0,) * len(shape), pipeline_mode=pl.Buffered(1))


def _whole_out(shape):
    return pl.BlockSpec(shape, lambda i: (0,) * len(shape))


_HBM = pl.BlockSpec(memory_space=pl.ANY)


def _params():
    return pltpu.CompilerParams(dimension_semantics=("arbitrary",), vmem_limit_bytes=VMEM_LIMIT_BYTES)


def _no_weights_needed(group):
    del group


def _ffn_kernel(x_ref, xs_ref, g_ref, wg_hbm, wu_hbm, wd_hbm, fg_ref, o_ref, os_ref,
                wg_s, wu_s, wd_s, stage_in, stage_out, sem, *, layer, final):
    step = pl.program_id(0)
    n_chunks = D_FF // FF_CHUNK

    def copies(c):
        slot = c % FFN_SLOTS
        cols = pl.ds(c * FF_CHUNK, FF_CHUNK)
        return (pltpu.make_async_copy(wg_hbm.at[layer, :, cols], stage_in.at[slot, 0], sem.at[slot, 0]),
                pltpu.make_async_copy(wu_hbm.at[layer, :, cols], stage_in.at[slot, 1], sem.at[slot, 1]),
                pltpu.make_async_copy(wd_hbm.at[layer, cols, :], stage_out.at[slot], sem.at[slot, 2]))

    def fetch(c):
        for cp in copies(c):
            cp.start()

    def land(c):
        slot = c % FFN_SLOTS
        sl = slice(c * FF_CHUNK, (c + 1) * FF_CHUNK)
        for cp in copies(c):
            cp.wait()
        wg_s[:, sl] = stage_in[slot, 0].astype(BF16)
        wu_s[:, sl] = stage_in[slot, 1].astype(BF16)
        wd_s[sl, :] = stage_out[slot].astype(BF16)
        if c + FFN_SLOTS < n_chunks:
            fetch(c + FFN_SLOTS)

    def half_step(x, need):
        xn = _rms_norm(x, g_ref[...]).astype(BF16)
        acc = None
        for c in range(n_chunks):
            need(c)
            sl = slice(c * FF_CHUNK, (c + 1) * FF_CHUNK)
            gate = _dot(xn, wg_s[:, sl])
            up = _dot(xn, wu_s[:, sl])
            a = (jax.nn.silu(gate) * up).astype(BF16)
            d = _dot(a, wd_s[sl, :])
            acc = d if acc is None else acc + d
        h = x + 0.5 * acc
        if final:
            h = _rms_norm(h, fg_ref[...])
        return h

    @pl.when(step == 0)
    def _():
        for c in range(FFN_SLOTS):
            fetch(c)
        o_ref[...] = half_step(x_ref[...], land)

    @pl.when(step != 0)
    def _():
        o_ref[...] = half_step(x_ref[...], _no_weights_needed)

    @pl.when(step == pl.num_programs(0) - 1)
    def _():
        os_ref[...] = half_step(xs_ref[...], _no_weights_needed)


def _ffn(x, xs, norm_g, wg, wu, wd, final_g, *, layer, final):
    rows, srows = x.shape[0], xs.shape[0]
    assert rows // ROW_TILE > 1
    row_spec = pl.BlockSpec((ROW_TILE, D_MODEL), lambda i: (i, 0))
    return pl.pallas_call(
        functools.partial(_ffn_kernel, layer=layer, final=final),
        grid=(rows // ROW_TILE,),
        in_specs=[row_spec, _resident((srows, D_MODEL)), _resident((1, D_MODEL)), _HBM, _HBM, _HBM,
                  _resident((1, D_MODEL))],
        out_specs=[row_spec, _whole_out((srows, D_MODEL))],
        out_shape=[jax.ShapeDtypeStruct((rows, D_MODEL), F32), jax.ShapeDtypeStruct((srows, D_MODEL), F32)],
        scratch_shapes=[pltpu.VMEM((D_MODEL, D_FF), BF16), pltpu.VMEM((D_MODEL, D_FF), BF16),
                        pltpu.VMEM((D_FF, D_MODEL), BF16),
                        pltpu.VMEM((FFN_SLOTS, 2, D_MODEL, FF_CHUNK), F32),
                        pltpu.VMEM((FFN_SLOTS, FF_CHUNK, D_MODEL), F32),
                        pltpu.SemaphoreType.DMA((FFN_SLOTS, 3))],
        compiler_params=_params(),
        name="ffn_final" if final else "ffn",
    )(x, xs, norm_g, wg, wu, wd, final_g)


def _causal_spatial(ws_ref, g):
    t = lax.broadcasted_iota(jnp.int32, (CHUNK, CHUNK), 0)
    s = lax.broadcasted_iota(jnp.int32, (CHUNK, CHUNK), 1)
    return jnp.where(s <= t, ws_ref[g], 0.0).astype(BF16)


_MIX_GROUPS = tuple(
    [tuple(("in", col + g * COL_BLOCK) for col in (COL_V, COL_B, COL_C, COL_X, COL_GB)) for g in range(G_A)]
    + [tuple(("in", col + g * COL_BLOCK) for g in pair for col in (COL_U, COL_GA)) for pair in ((0, 1), (2, 3))]
    + [tuple(("out", j * COL_BLOCK) for j in range(D_MODEL // COL_BLOCK))])
MIX_GROUP_PIECES = max(len(group) for group in _MIX_GROUPS)


def _mix_kernel(h_ref, hs_ref, s0_ref, s1_ref, ng_ref, win_hbm, bin_ref, lng_ref, lnb_ref, ws_ref, bs_ref, cw_ref,
                wout_hbm, o_ref, buf_ref, os_ref, xgs_ref, vs_ref,
                win_s, wout_s, stage, sem, vn_s, xg_s, yb_s, m_s, ms_s, *, layer, tm, tiles_per_seq):
    step = pl.program_id(0)

    def copies(k):
        slot = k % MIX_SLOTS
        return [pltpu.make_async_copy((win_hbm if which == "in" else wout_hbm).at[layer, :, pl.ds(col, COL_BLOCK)],
                                      stage.at[slot, j], sem.at[slot, j])
                for j, (which, col) in enumerate(_MIX_GROUPS[k])]

    def fetch(k):
        for cp in copies(k):
            cp.start()

    def land(k):
        slot = k % MIX_SLOTS
        for cp in copies(k):
            cp.wait()
        for j, (which, col) in enumerate(_MIX_GROUPS[k]):
            dst = win_s if which == "in" else wout_s
            dst[:, col:col + COL_BLOCK] = stage[slot, j].astype(BF16)
        if k + MIX_SLOTS < len(_MIX_GROUPS):
            fetch(k + MIX_SLOTS)

    def tile(need):
        h = h_ref[...]
        n = _rms_norm(h, ng_ref[...]).astype(BF16)

        def proj(col):
            return _dot(n, win_s[:, col:col + COL_BLOCK]) + bin_ref[:, col:col + COL_BLOCK]

        vs = []
        for g in range(G_A):
            need(g)
            c0 = g * COL_BLOCK
            cs = slice(c0, c0 + COL_BLOCK)
            vs.append(_gelu(proj(COL_V + c0)))
            gate_b = proj(COL_B + c0)
            xg = proj(COL_C + c0) * proj(COL_X + c0)
            xg_s[HALO:HALO + tm, cs] = xg
            conv = cw_ref[0:1, cs] * xg_s[HALO - 2:HALO - 2 + tm, cs]
            conv = conv + cw_ref[1:2, cs] * xg_s[HALO - 1:HALO - 1 + tm, cs]
            conv = conv + cw_ref[2:3, cs] * xg
            y_b = gate_b * conv
            yb_s[:, cs] = jax.nn.sigmoid(proj(COL_GB + c0)) * y_b

        mu = sum(jnp.sum(v, axis=-1, keepdims=True) for v in vs) / E_A
        vs = [v - mu for v in vs]
        var = sum(jnp.sum(v * v, axis=-1, keepdims=True) for v in vs) / E_A
        rstd = lax.rsqrt(var + EPS)
        for g in range(G_A):
            cs = slice(g * COL_BLOCK, (g + 1) * COL_BLOCK)
            vn_s[:, cs] = ((vs[g] * rstd) * lng_ref[:, cs] + lnb_ref[:, cs]).astype(BF16)

        for g in range(G_A):
            if g % 2 == 0:
                need(G_A + g // 2)
            c0 = g * COL_BLOCK
            cs = slice(c0, c0 + COL_BLOCK)
            u = _gelu(proj(COL_U + c0))
            w_mix = _causal_spatial(ws_ref, g)
            z = jnp.concatenate(
                [_dot(w_mix, vn_s[c * CHUNK:(c + 1) * CHUNK, cs]) + bs_ref[g] for c in range(tm // CHUNK)], axis=0)
            y_a = u * z
            m = jax.nn.sigmoid(proj(COL_GA + c0)) * y_a + yb_s[:, cs]
            m_s[:, cs] = m.astype(BF16)

        last = xg_s[HALO + tm - 2:HALO + tm, :]
        buf_ref[0] = last
        xg_s[HALO - 2:HALO, :] = last
        need(len(_MIX_GROUPS) - 1)
        o_ref[...] = h + _dot(m_s[...], wout_s[...])

    @pl.when(step % tiles_per_seq == 0)
    def _():
        xg_s[0:HALO, :] = jnp.zeros((HALO, E_B), F32)

    @pl.when(step == 0)
    def _():
        for k in range(MIX_SLOTS):
            fetch(k)
        tile(land)

    @pl.when(step != 0)
    def _():
        tile(_no_weights_needed)

    @pl.when(step == pl.num_programs(0) - 1)
    def _():
        hs = hs_ref[...]
        ns = _rms_norm(hs, ng_ref[...]).astype(BF16)

        def proj_s(col, width=COL_BLOCK):
            return _dot(ns, win_s[:, col:col + width]) + bin_ref[:, col:col + width]

        v = _layer_norm(_gelu(proj_s(COL_V, E_A)), lng_ref[...], lnb_ref[...])
        vs_ref[...] = v
        for g in range(G_A):
            c0 = g * COL_BLOCK
            cs = slice(c0, c0 + COL_BLOCK)
            u = _gelu(proj_s(COL_U + c0))
            z = ws_ref[g][0:1, 0:1] * v[:, cs] + bs_ref[g][0:1, :]
            y_a = u * z
            gate_b = proj_s(COL_B + c0)
            xg = proj_s(COL_C + c0) * proj_s(COL_X + c0)
            xgs_ref[:, cs] = xg
            conv = cw_ref[0:1, cs] * s0_ref[:, cs]
            conv = conv + cw_ref[1:2, cs] * s1_ref[:, cs]
            conv = conv + cw_ref[2:3, cs] * xg
            y_b = gate_b * conv
            m = jax.nn.sigmoid(proj_s(COL_GA + c0)) * y_a + jax.nn.sigmoid(proj_s(COL_GB + c0)) * y_b
            ms_s[:, cs] = m.astype(BF16)
        os_ref[...] = hs + _dot(ms_s[...], wout_s[...])


def _mix(h, hs, s0, s1, norm_g, w_in, b_in, ln_g, ln_b, w_s, b_s, conv_w, w_out, *, layer, seq):
    rows, srows = h.shape[0], hs.shape[0]
    tm = ROW_TILE
    tiles_per_seq = seq // tm
    assert rows // tm > 1
    row_spec = pl.BlockSpec((tm, D_MODEL), lambda i: (i, 0))
    sample = _resident((srows, D_MODEL))
    sample_out = _whole_out((srows, D_MODEL))
    return pl.pallas_call(
        functools.partial(_mix_kernel, layer=layer, tm=tm, tiles_per_seq=tiles_per_seq),
        grid=(rows // tm,),
        in_specs=[row_spec, sample, sample, sample, _resident((1, D_MODEL)), _HBM, _resident((1, IN_COLS)),
                  _resident((1, E_A)), _resident((1, E_A)), _resident((G_A, CHUNK, CHUNK)),
                  _resident((G_A, CHUNK, 1)), _resident((CONV_W, E_B)), _HBM],
        out_specs=[row_spec, pl.BlockSpec((1, CONV_W - 1, E_B), lambda i: (i // tiles_per_seq, 0, 0)),
                   sample_out, sample_out, sample_out],
        out_shape=[jax.ShapeDtypeStruct((rows, D_MODEL), F32),
                   jax.ShapeDtypeStruct((rows // seq, CONV_W - 1, E_B), F32)]
                  + [jax.ShapeDtypeStruct((srows, D_MODEL), F32)] * 3,
        scratch_shapes=[pltpu.VMEM((D_MODEL, IN_COLS), BF16), pltpu.VMEM((D_MODEL, D_MODEL), BF16),
                        pltpu.VMEM((MIX_SLOTS, MIX_GROUP_PIECES, D_MODEL, COL_BLOCK), F32),
                        pltpu.SemaphoreType.DMA((MIX_SLOTS, MIX_GROUP_PIECES)),
                        pltpu.VMEM((tm, E_A), BF16), pltpu.VMEM((HALO + tm, E_B), F32),
                        pltpu.VMEM((tm, E_B), F32), pltpu.VMEM((tm, D_MODEL), BF16),
                        pltpu.VMEM((srows, D_MODEL), BF16)],
        compiler_params=_params(),
        name="mix",
    )(h, hs, s0, s1, norm_g, w_in, b_in, ln_g, ln_b, w_s, b_s, conv_w, w_out)


def kernel(x_prompt, x_sample, state_conv, ffn1_norm, ffn1_w_gate, ffn1_w_up, ffn1_w_down, mix_norm, w_in, b_in,
           v_ln_gain, v_ln_bias, w_spatial, b_spatial, conv_w, w_out, ffn2_norm, ffn2_w_gate, ffn2_w_up,
           ffn2_w_down, final_norm):
    batch, seq, _ = x_prompt.shape
    dec_batch, dec_seq, _ = x_sample.shape
    depth = w_in.shape[0]
    assert dec_seq == 1 and seq % ROW_TILE == 0 and ROW_TILE % CHUNK == 0

    hp = x_prompt.reshape(batch * seq, D_MODEL)
    hs = x_sample.reshape(dec_batch, D_MODEL)
    fin_g = final_norm.reshape(1, D_MODEL)
    row = lambda a: a.reshape(1, -1)
    conv_p, conv_s, v_s = [], [], []
    for l in range(depth):
        last = l == depth - 1
        hp, hs = _ffn(hp, hs, row(ffn1_norm[l]), ffn1_w_gate, ffn1_w_up, ffn1_w_down, fin_g, layer=l, final=False)
        hp, buf_p, hs, xg_s, vn_s = _mix(
            hp, hs, state_conv[l, :, 0], state_conv[l, :, 1], row(mix_norm[l]), w_in, row(b_in[l]),
            row(v_ln_gain[l]), row(v_ln_bias[l]), w_spatial[l], b_spatial[l][:, :, None], conv_w[l], w_out,
            layer=l, seq=seq)
        hp, hs = _ffn(hp, hs, row(ffn2_norm[l]), ffn2_w_gate, ffn2_w_up, ffn2_w_down, fin_g, layer=l, final=last)

        conv_p.append(buf_p)
        conv_s.append(jnp.stack([state_conv[l, :, 1], xg_s], axis=1))
        v_s.append(vn_s.reshape(dec_batch, dec_seq, E_A))

    return (hp.reshape(batch, seq, D_MODEL), hs.reshape(dec_batch, dec_seq, D_MODEL),
            jnp.stack(conv_p, axis=0), jnp.stack(conv_s, axis=0), jnp.stack(v_s, axis=0))
```

```python
import functools

import jax
import jax.numpy as jnp
import numpy as np
from jax import lax
from jax.experimental import pallas as pl
from jax.experimental.pallas import tpu as pltpu

D_MODEL = 1024
E_A = D_MODEL
G_A = 4
HD_A = E_A // G_A
CHUNK = 128
E_B = D_MODEL
CONV_W = 3
D_FF = 2816
EPS = 1e-6
SQRT_HALF = float(np.sqrt(0.5).astype(np.float32))
IN_COLS = 2 * E_A + 3 * E_B + 2 * D_MODEL
COL_U, COL_V, COL_B, COL_C, COL_X, COL_GA, COL_GB = (k * D_MODEL for k in range(7))

ROW_TILE = 512
FF_CHUNK = 256
COL_BLOCK = HD_A
HALO = 8
FFN_SLOTS = 2
ROWS_W_IN = 64
ROWS_W_OUT = 256
VMEM_LIMIT_BYTES = 56 * 1024 * 1024

F32 = jnp.float32
BF16 = jnp.bfloat16


def _rms_norm(x, g):
    return (x * lax.rsqrt(jnp.mean(x * x, axis=-1, keepdims=True) + EPS)) * g


def _layer_norm(x, g, b):
    mu = jnp.mean(x, axis=-1, keepdims=True)
    xc = x - mu
    return (xc * lax.rsqrt(jnp.mean(xc * xc, axis=-1, keepdims=True) + EPS)) * g + b


def _gelu(x):
    return 0.5 * x * (1.0 + lax.erf(x * SQRT_HALF))


def _dot(a, b):
    return jnp.dot(a, b, preferred_element_type=F32)


def _resident(shape):
    return pl.BlockSpec(shape, lambda i: (0,) * len(shape), pipeline_mode=pl.Buffered(1))


def _whole_out(shape):
    return pl.BlockSpec(shape, lambda i: (0,) * len(shape))


_HBM = pl.BlockSpec(memory_space=pl.ANY)


def _params():
    return pltpu.CompilerParams(dimension_semantics=("arbitrary",), vmem_limit_bytes=VMEM_LIMIT_BYTES)


def _no_weights_needed(chunk):
    del chunk


def _stream_cast(src, dst, stage, sem):
    rows = stage.shape[1]
    n = src.shape[0] // rows
    assert n * rows == src.shape[0]

    def copy(k, slot):
        return pltpu.make_async_copy(src.at[pl.ds(k * rows, rows), :], stage.at[slot], sem.at[slot])

    copy(0, 0).start()

    def body(k, carry):
        slot = lax.rem(k, 2)

        @pl.when(k + 1 < n)
        def _():
            copy(k + 1, 1 - slot).start()

        copy(k, slot).wait()
        dst[pl.ds(pl.multiple_of(k * rows, rows), rows), :] = stage[slot].astype(BF16)
        return carry

    lax.fori_loop(0, n, body, 0)


def _ffn_kernel(x_ref, xs_ref, g_ref, wg_hbm, wu_hbm, wd_hbm, fg_ref, o_ref, os_ref,
                wg_s, wu_s, wd_s, stage_in, stage_out, sem, *, layer, final):
    step = pl.program_id(0)
    n_chunks = D_FF // FF_CHUNK

    def copies(c):
        slot = c % FFN_SLOTS
        cols = pl.ds(c * FF_CHUNK, FF_CHUNK)
        return (pltpu.make_async_copy(wg_hbm.at[layer, :, cols], stage_in.at[slot, 0], sem.at[slot, 0]),
                pltpu.make_async_copy(wu_hbm.at[layer, :, cols], stage_in.at[slot, 1], sem.at[slot, 1]),
                pltpu.make_async_copy(wd_hbm.at[layer, cols, :], stage_out.at[slot], sem.at[slot, 2]))

    def fetch(c):
        for cp in copies(c):
            cp.start()

    def land(c):
        slot = c % FFN_SLOTS
        sl = slice(c * FF_CHUNK, (c + 1) * FF_CHUNK)
        for cp in copies(c):
            cp.wait()
        wg_s[:, sl] = stage_in[slot, 0].astype(BF16)
        wu_s[:, sl] = stage_in[slot, 1].astype(BF16)
        wd_s[sl, :] = stage_out[slot].astype(BF16)
        if c + FFN_SLOTS < n_chunks:
            fetch(c + FFN_SLOTS)

    def half_step(x, need):
        xn = _rms_norm(x, g_ref[...]).astype(BF16)
        acc = None
        for c in range(n_chunks):
            need(c)
            sl = slice(c * FF_CHUNK, (c + 1) * FF_CHUNK)
            gate = _dot(xn, wg_s[:, sl])
            up = _dot(xn, wu_s[:, sl])
            a = (jax.nn.silu(gate) * up).astype(BF16)
            d = _dot(a, wd_s[sl, :])
            acc = d if acc is None else acc + d
        h = x + 0.5 * acc
        if final:
            h = _rms_norm(h, fg_ref[...])
        return h

    @pl.when(step == 0)
    def _():
        for c in range(FFN_SLOTS):
            fetch(c)
        o_ref[...] = half_step(x_ref[...], land)

    @pl.when(step != 0)
    def _():
        o_ref[...] = half_step(x_ref[...], _no_weights_needed)

    @pl.when(step == pl.num_programs(0) - 1)
    def _():
        os_ref[...] = half_step(xs_ref[...], _no_weights_needed)


def _ffn(x, xs, norm_g, wg, wu, wd, final_g, *, layer, final):
    rows, srows = x.shape[0], xs.shape[0]
    assert rows // ROW_TILE > 1
    row_spec = pl.BlockSpec((ROW_TILE, D_MODEL), lambda i: (i, 0))
    return pl.pallas_call(
        functools.partial(_ffn_kernel, layer=layer, final=final),
        grid=(rows // ROW_TILE,),
        in_specs=[row_spec, _resident((srows, D_MODEL)), _resident((1, D_MODEL)), _HBM, _HBM, _HBM,
                  _resident((1, D_MODEL))],
        out_specs=[row_spec, _whole_out((srows, D_MODEL))],
        out_shape=[jax.ShapeDtypeStruct((rows, D_MODEL), F32), jax.ShapeDtypeStruct((srows, D_MODEL), F32)],
        scratch_shapes=[pltpu.VMEM((D_MODEL, D_FF), BF16), pltpu.VMEM((D_MODEL, D_FF), BF16),
                        pltpu.VMEM((D_FF, D_MODEL), BF16),
                        pltpu.VMEM((FFN_SLOTS, 2, D_MODEL, FF_CHUNK), F32),
                        pltpu.VMEM((FFN_SLOTS, FF_CHUNK, D_MODEL), F32),
                        pltpu.SemaphoreType.DMA((FFN_SLOTS, 3))],
        compiler_params=_params(),
        name="ffn_final" if final else "ffn",
    )(x, xs, norm_g, wg, wu, wd, final_g)


def _causal_spatial(ws_ref, g):
    t = lax.broadcasted_iota(jnp.int32, (CHUNK, CHUNK), 0)
    s = lax.broadcasted_iota(jnp.int32, (CHUNK, CHUNK), 1)
    return jnp.where(s <= t, ws_ref[g], 0.0).astype(BF16)


def _mix_kernel(h_ref, hs_ref, s0_ref, s1_ref, ng_ref, win_hbm, bin_ref, lng_ref, lnb_ref, ws_ref, bs_ref, cw_ref,
                wout_hbm, o_ref, buf_ref, os_ref, xgs_ref, vs_ref,
                win_s, wout_s, stage_in, stage_out, sem, vn_s, xg_s, yb_s, m_s, ms_s, *, layer, tm, tiles_per_seq):
    step = pl.program_id(0)

    @pl.when(step == 0)
    def _():
        _stream_cast(win_hbm.at[layer], win_s, stage_in, sem)
        _stream_cast(wout_hbm.at[layer], wout_s, stage_out, sem)

    @pl.when(step % tiles_per_seq == 0)
    def _():
        xg_s[0:HALO, :] = jnp.zeros((HALO, E_B), F32)

    h = h_ref[...]
    n = _rms_norm(h, ng_ref[...]).astype(BF16)

    def proj(col, width=COL_BLOCK):
        return _dot(n, win_s[:, col:col + width]) + bin_ref[:, col:col + width]

    vs = []
    for g in range(G_A):
        c0 = g * COL_BLOCK
        cs = slice(c0, c0 + COL_BLOCK)
        vs.append(_gelu(proj(COL_V + c0)))
        gate_b = proj(COL_B + c0)
        xg = proj(COL_C + c0) * proj(COL_X + c0)
        xg_s[HALO:HALO + tm, cs] = xg
        conv = cw_ref[0:1, cs] * xg_s[HALO - 2:HALO - 2 + tm, cs]
        conv = conv + cw_ref[1:2, cs] * xg_s[HALO - 1:HALO - 1 + tm, cs]
        conv = conv + cw_ref[2:3, cs] * xg
        y_b = gate_b * conv
        yb_s[:, cs] = jax.nn.sigmoid(proj(COL_GB + c0)) * y_b

    mu = sum(jnp.sum(v, axis=-1, keepdims=True) for v in vs) / E_A
    vs = [v - mu for v in vs]
    var = sum(jnp.sum(v * v, axis=-1, keepdims=True) for v in vs) / E_A
    rstd = lax.rsqrt(var + EPS)
    for g in range(G_A):
        cs = slice(g * COL_BLOCK, (g + 1) * COL_BLOCK)
        vn_s[:, cs] = ((vs[g] * rstd) * lng_ref[:, cs] + lnb_ref[:, cs]).astype(BF16)

    for g in range(G_A):
        c0 = g * COL_BLOCK
        cs = slice(c0, c0 + COL_BLOCK)
        u = _gelu(proj(COL_U + c0))
        w_mix = _causal_spatial(ws_ref, g)
        z = jnp.concatenate(
            [_dot(w_mix, vn_s[c * CHUNK:(c + 1) * CHUNK, cs]) + bs_ref[g] for c in range(tm // CHUNK)], axis=0)
        y_a = u * z
        m = jax.nn.sigmoid(proj(COL_GA + c0)) * y_a + yb_s[:, cs]
        m_s[:, cs] = m.astype(BF16)

    last = xg_s[HALO + tm - 2:HALO + tm, :]
    buf_ref[0] = last
    xg_s[HALO - 2:HALO, :] = last
    o_ref[...] = h + _dot(m_s[...], wout_s[...])

    @pl.when(step == pl.num_programs(0) - 1)
    def _():
        hs = hs_ref[...]
        ns = _rms_norm(hs, ng_ref[...]).astype(BF16)

        def proj_s(col, width=COL_BLOCK):
            return _dot(ns, win_s[:, col:col + width]) + bin_ref[:, col:col + width]

        v = _layer_norm(_gelu(proj_s(COL_V, E_A)), lng_ref[...], lnb_ref[...])
        vs_ref[...] = v
        for g in range(G_A):
            c0 = g * COL_BLOCK
            cs = slice(c0, c0 + COL_BLOCK)
            u = _gelu(proj_s(COL_U + c0))
            z = ws_ref[g][0:1, 0:1] * v[:, cs] + bs_ref[g][0:1, :]
            y_a = u * z
            gate_b = proj_s(COL_B + c0)
            xg = proj_s(COL_C + c0) * proj_s(COL_X + c0)
            xgs_ref[:, cs] = xg
            conv = cw_ref[0:1, cs] * s0_ref[:, cs]
            conv = conv + cw_ref[1:2, cs] * s1_ref[:, cs]
            conv = conv + cw_ref[2:3, cs] * xg
            y_b = gate_b * conv
            m = jax.nn.sigmoid(proj_s(COL_GA + c0)) * y_a + jax.nn.sigmoid(proj_s(COL_GB + c0)) * y_b
            ms_s[:, cs] = m.astype(BF16)
        os_ref[...] = hs + _dot(ms_s[...], wout_s[...])


def _mix(h, hs, s0, s1, norm_g, w_in, b_in, ln_g, ln_b, w_s, b_s, conv_w, w_out, *, layer, seq):
    rows, srows = h.shape[0], hs.shape[0]
    tm = ROW_TILE
    tiles_per_seq = seq // tm
    row_spec = pl.BlockSpec((tm, D_MODEL), lambda i: (i, 0))
    sample = _resident((srows, D_MODEL))
    sample_out = _whole_out((srows, D_MODEL))
    return pl.pallas_call(
        functools.partial(_mix_kernel, layer=layer, tm=tm, tiles_per_seq=tiles_per_seq),
        grid=(rows // tm,),
        in_specs=[row_spec, sample, sample, sample, _resident((1, D_MODEL)), _HBM, _resident((1, IN_COLS)),
                  _resident((1, E_A)), _resident((1, E_A)), _resident((G_A, CHUNK, CHUNK)),
                  _resident((G_A, CHUNK, 1)), _resident((CONV_W, E_B)), _HBM],
        out_specs=[row_spec, pl.BlockSpec((1, CONV_W - 1, E_B), lambda i: (i // tiles_per_seq, 0, 0)),
                   sample_out, sample_out, sample_out],
        out_shape=[jax.ShapeDtypeStruct((rows, D_MODEL), F32),
                   jax.ShapeDtypeStruct((rows // seq, CONV_W - 1, E_B), F32)]
                  + [jax.ShapeDtypeStruct((srows, D_MODEL), F32)] * 3,
        scratch_shapes=[pltpu.VMEM((D_MODEL, IN_COLS), BF16), pltpu.VMEM((D_MODEL, D_MODEL), BF16),
                        pltpu.VMEM((2, ROWS_W_IN, IN_COLS), F32), pltpu.VMEM((2, ROWS_W_OUT, D_MODEL), F32),
                        pltpu.SemaphoreType.DMA((2,)),
                        pltpu.VMEM((tm, E_A), BF16), pltpu.VMEM((HALO + tm, E_B), F32),
                        pltpu.VMEM((tm, E_B), F32), pltpu.VMEM((tm, D_MODEL), BF16),
                        pltpu.VMEM((srows, D_MODEL), BF16)],
        compiler_params=_params(),
        name="mix",
    )(h, hs, s0, s1, norm_g, w_in, b_in, ln_g, ln_b, w_s, b_s, conv_w, w_out)


def kernel(x_prompt, x_sample, state_conv, ffn1_norm, ffn1_w_gate, ffn1_w_up, ffn1_w_down, mix_norm, w_in, b_in,
           v_ln_gain, v_ln_bias, w_spatial, b_spatial, conv_w, w_out, ffn2_norm, ffn2_w_gate, ffn2_w_up,
           ffn2_w_down, final_norm):
    batch, seq, _ = x_prompt.shape
    dec_batch, dec_seq, _ = x_sample.shape
    depth = w_in.shape[0]
    assert dec_seq == 1 and seq % ROW_TILE == 0 and ROW_TILE % CHUNK == 0

    hp = x_prompt.reshape(batch * seq, D_MODEL)
    hs = x_sample.reshape(dec_batch, D_MODEL)
    fin_g = final_norm.reshape(1, D_MODEL)
    row = lambda a: a.reshape(1, -1)
    conv_p, conv_s, v_s = [], [], []
    for l in range(depth):
        last = l == depth - 1
        hp, hs = _ffn(hp, hs, row(ffn1_norm[l]), ffn1_w_gate, ffn1_w_up, ffn1_w_down, fin_g, layer=l, final=False)
        hp, buf_p, hs, xg_s, vn_s = _mix(
            hp, hs, state_conv[l, :, 0], state_conv[l, :, 1], row(mix_norm[l]), w_in, row(b_in[l]),
            row(v_ln_gain[l]), row(v_ln_bias[l]), w_spatial[l], b_spatial[l][:, :, None], conv_w[l], w_out,
            layer=l, seq=seq)
        hp, hs = _ffn(hp, hs, row(ffn2_norm[l]), ffn2_w_gate, ffn2_w_up, ffn2_w_down, fin_g, layer=l, final=last)

        conv_p.append(buf_p)
        conv_s.append(jnp.stack([state_conv[l, :, 1], xg_s], axis=1))
        v_s.append(vn_s.reshape(dec_batch, dec_seq, E_A))

    return (hp.reshape(batch, seq, D_MODEL), hs.reshape(dec_batch, dec_seq, D_MODEL),
            jnp.stack(conv_p, axis=0), jnp.stack(conv_s, axis=0), jnp.stack(v_s, axis=0))
```

```python
import functools

import jax
import jax.numpy as jnp
import numpy as np
from jax import lax
from jax.experimental import pallas as pl
from jax.experimental.pallas import tpu as pltpu

D_MODEL = 1024
E_A = D_MODEL
G_A = 4
HD_A = E_A // G_A
CHUNK = 128
E_B = D_MODEL
CONV_W = 3
D_FF = 2816
EPS = 1e-6
SQRT_HALF = float(np.sqrt(0.5).astype(np.float32))
IN_COLS = 2 * E_A + 3 * E_B + 2 * D_MODEL
COL_U, COL_V, COL_B, COL_C, COL_X, COL_GA, COL_GB = (k * D_MODEL for k in range(7))

ROW_TILE = 512
FFN_ROW_TILE = 1024
FF_CHUNK = 256
COL_BLOCK = HD_A
HALO = 8
FFN_SLOTS = 2
ROWS_W_IN = 64
ROWS_W_OUT = 256
VMEM_LIMIT_BYTES = 56 * 1024 * 1024

F32 = jnp.float32
BF16 = jnp.bfloat16


def _rms_norm(x, g):
    return (x * lax.rsqrt(jnp.mean(x * x, axis=-1, keepdims=True) + EPS)) * g


def _layer_norm(x, g, b):
    mu = jnp.mean(x, axis=-1, keepdims=True)
    xc = x - mu
    return (xc * lax.rsqrt(jnp.mean(xc * xc, axis=-1, keepdims=True) + EPS)) * g + b


def _gelu(x):
    return 0.5 * x * (1.0 + lax.erf(x * SQRT_HALF))


def _dot(a, b):
    return jnp.dot(a, b, preferred_element_type=F32)


def _resident(shape):
    return pl.BlockSpec(shape, lambda i: (0,) * len(shape), pipeline_mode=pl.Buffered(1))


def _whole_out(shape):
    return pl.BlockSpec(shape, lambda i: (0,) * len(shape))


_HBM = pl.BlockSpec(memory_space=pl.ANY)


def _params():
    return pltpu.CompilerParams(dimension_semantics=("arbitrary",), vmem_limit_bytes=VMEM_LIMIT_BYTES)


def _no_weights_needed(chunk):
    del chunk


def _stream_cast(src, dst, stage, sem):
    rows = stage.shape[1]
    n = src.shape[0] // rows
    assert n * rows == src.shape[0]

    def copy(k, slot):
        return pltpu.make_async_copy(src.at[pl.ds(k * rows, rows), :], stage.at[slot], sem.at[slot])

    copy(0, 0).start()

    def body(k, carry):
        slot = lax.rem(k, 2)

        @pl.when(k + 1 < n)
        def _():
            copy(k + 1, 1 - slot).start()

        copy(k, slot).wait()
        dst[pl.ds(pl.multiple_of(k * rows, rows), rows), :] = stage[slot].astype(BF16)
        return carry

    lax.fori_loop(0, n, body, 0)


def _ffn_kernel(x_ref, xs_ref, g_ref, wg_hbm, wu_hbm, wd_hbm, fg_ref, o_ref, os_ref,
                wg_s, wu_s, wd_s, stage_in, stage_out, sem, a_s, *, layer, final):
    step = pl.program_id(0)
    n_chunks = D_FF // FF_CHUNK

    def copies(c):
        slot = c % FFN_SLOTS
        cols = pl.ds(c * FF_CHUNK, FF_CHUNK)
        return (pltpu.make_async_copy(wg_hbm.at[layer, :, cols], stage_in.at[slot, 0], sem.at[slot, 0]),
                pltpu.make_async_copy(wu_hbm.at[layer, :, cols], stage_in.at[slot, 1], sem.at[slot, 1]),
                pltpu.make_async_copy(wd_hbm.at[layer, cols, :], stage_out.at[slot], sem.at[slot, 2]))

    def fetch(c):
        for cp in copies(c):
            cp.start()

    def land(c):
        slot = c % FFN_SLOTS
        sl = slice(c * FF_CHUNK, (c + 1) * FF_CHUNK)
        for cp in copies(c):
            cp.wait()
        wg_s[:, sl] = stage_in[slot, 0].astype(BF16)
        wu_s[:, sl] = stage_in[slot, 1].astype(BF16)
        wd_s[sl, :] = stage_out[slot].astype(BF16)
        if c + FFN_SLOTS < n_chunks:
            fetch(c + FFN_SLOTS)

    def half_step(x, need):
        rows = x.shape[0]
        xn = _rms_norm(x, g_ref[...]).astype(BF16)
        for c in range(n_chunks):
            need(c)
            sl = slice(c * FF_CHUNK, (c + 1) * FF_CHUNK)
            gate = _dot(xn, wg_s[:, sl])
            up = _dot(xn, wu_s[:, sl])
            a_s[0:rows, sl] = (jax.nn.silu(gate) * up).astype(BF16)
        h = x + 0.5 * _dot(a_s[0:rows, :], wd_s[...])
        if final:
            h = _rms_norm(h, fg_ref[...])
        return h

    @pl.when(step == 0)
    def _():
        for c in range(FFN_SLOTS):
            fetch(c)
        o_ref[...] = half_step(x_ref[...], land)

    @pl.when(step != 0)
    def _():
        o_ref[...] = half_step(x_ref[...], _no_weights_needed)

    @pl.when(step == pl.num_programs(0) - 1)
    def _():
        os_ref[...] = half_step(xs_ref[...], _no_weights_needed)


def _ffn(x, xs, norm_g, wg, wu, wd, final_g, *, layer, final):
    rows, srows = x.shape[0], xs.shape[0]
    assert rows // FFN_ROW_TILE > 1
    row_spec = pl.BlockSpec((FFN_ROW_TILE, D_MODEL), lambda i: (i, 0))
    return pl.pallas_call(
        functools.partial(_ffn_kernel, layer=layer, final=final),
        grid=(rows // FFN_ROW_TILE,),
        in_specs=[row_spec, _resident((srows, D_MODEL)), _resident((1, D_MODEL)), _HBM, _HBM, _HBM,
                  _resident((1, D_MODEL))],
        out_specs=[row_spec, _whole_out((srows, D_MODEL))],
        out_shape=[jax.ShapeDtypeStruct((rows, D_MODEL), F32), jax.ShapeDtypeStruct((srows, D_MODEL), F32)],
        scratch_shapes=[pltpu.VMEM((D_MODEL, D_FF), BF16), pltpu.VMEM((D_MODEL, D_FF), BF16),
                        pltpu.VMEM((D_FF, D_MODEL), BF16),
                        pltpu.VMEM((FFN_SLOTS, 2, D_MODEL, FF_CHUNK), F32),
                        pltpu.VMEM((FFN_SLOTS, FF_CHUNK, D_MODEL), F32),
                        pltpu.SemaphoreType.DMA((FFN_SLOTS, 3)), pltpu.VMEM((FFN_ROW_TILE, D_FF), BF16)],
        compiler_params=_params(),
        name="ffn_final" if final else "ffn",
    )(x, xs, norm_g, wg, wu, wd, final_g)


def _causal_spatial(ws_ref, g):
    t = lax.broadcasted_iota(jnp.int32, (CHUNK, CHUNK), 0)
    s = lax.broadcasted_iota(jnp.int32, (CHUNK, CHUNK), 1)
    return jnp.where(s <= t, ws_ref[g], 0.0).astype(BF16)


def _mix_kernel(h_ref, hs_ref, s0_ref, s1_ref, ng_ref, win_hbm, bin_ref, lng_ref, lnb_ref, ws_ref, bs_ref, cw_ref,
                wout_hbm, o_ref, buf_ref, os_ref, xgs_ref, vs_ref,
                win_s, wout_s, stage_in, stage_out, sem, vn_s, xg_s, yb_s, m_s, ms_s, *, layer, tm, tiles_per_seq):
    step = pl.program_id(0)

    @pl.when(step == 0)
    def _():
        _stream_cast(win_hbm.at[layer], win_s, stage_in, sem)
        _stream_cast(wout_hbm.at[layer], wout_s, stage_out, sem)

    @pl.when(step % tiles_per_seq == 0)
    def _():
        xg_s[0:HALO, :] = jnp.zeros((HALO, E_B), F32)

    h = h_ref[...]
    n = _rms_norm(h, ng_ref[...]).astype(BF16)

    def proj(col, width=COL_BLOCK):
        return _dot(n, win_s[:, col:col + width]) + bin_ref[:, col:col + width]

    vs = []
    for g in range(G_A):
        c0 = g * COL_BLOCK
        cs = slice(c0, c0 + COL_BLOCK)
        vs.append(_gelu(proj(COL_V + c0)))
        gate_b = proj(COL_B + c0)
        xg = proj(COL_C + c0) * proj(COL_X + c0)
        xg_s[HALO:HALO + tm, cs] = xg
        conv = cw_ref[0:1, cs] * xg_s[HALO - 2:HALO - 2 + tm, cs]
        conv = conv + cw_ref[1:2, cs] * xg_s[HALO - 1:HALO - 1 + tm, cs]
        conv = conv + cw_ref[2:3, cs] * xg
        y_b = gate_b * conv
        yb_s[:, cs] = jax.nn.sigmoid(proj(COL_GB + c0)) * y_b

    mu = sum(jnp.sum(v, axis=-1, keepdims=True) for v in vs) / E_A
    vs = [v - mu for v in vs]
    var = sum(jnp.sum(v * v, axis=-1, keepdims=True) for v in vs) / E_A
    rstd = lax.rsqrt(var + EPS)
    for g in range(G_A):
        cs = slice(g * COL_BLOCK, (g + 1) * COL_BLOCK)
        vn_s[:, cs] = ((vs[g] * rstd) * lng_ref[:, cs] + lnb_ref[:, cs]).astype(BF16)

    for g in range(G_A):
        c0 = g * COL_BLOCK
        cs = slice(c0, c0 + COL_BLOCK)
        u = _gelu(proj(COL_U + c0))
        w_mix = _causal_spatial(ws_ref, g)
        z = jnp.concatenate(
            [_dot(w_mix, vn_s[c * CHUNK:(c + 1) * CHUNK, cs]) + bs_ref[g] for c in range(tm // CHUNK)], axis=0)
        y_a = u * z
        m = jax.nn.sigmoid(proj(COL_GA + c0)) * y_a + yb_s[:, cs]
        m_s[:, cs] = m.astype(BF16)

    last = xg_s[HALO + tm - 2:HALO + tm, :]
    buf_ref[0] = last
    xg_s[HALO - 2:HALO, :] = last
    o_ref[...] = h + _dot(m_s[...], wout_s[...])

    @pl.when(step == pl.num_programs(0) - 1)
    def _():
        hs = hs_ref[...]
        ns = _rms_norm(hs, ng_ref[...]).astype(BF16)

        def proj_s(col, width=COL_BLOCK):
            return _dot(ns, win_s[:, col:col + width]) + bin_ref[:, col:col + width]

        v = _layer_norm(_gelu(proj_s(COL_V, E_A)), lng_ref[...], lnb_ref[...])
        vs_ref[...] = v
        for g in range(G_A):
            c0 = g * COL_BLOCK
            cs = slice(c0, c0 + COL_BLOCK)
            u = _gelu(proj_s(COL_U + c0))
            z = ws_ref[g][0:1, 0:1] * v[:, cs] + bs_ref[g][0:1, :]
            y_a = u * z
            gate_b = proj_s(COL_B + c0)
            xg = proj_s(COL_C + c0) * proj_s(COL_X + c0)
            xgs_ref[:, cs] = xg
            conv = cw_ref[0:1, cs] * s0_ref[:, cs]
            conv = conv + cw_ref[1:2, cs] * s1_ref[:, cs]
            conv = conv + cw_ref[2:3, cs] * xg
            y_b = gate_b * conv
            m = jax.nn.sigmoid(proj_s(COL_GA + c0)) * y_a + jax.nn.sigmoid(proj_s(COL_GB + c0)) * y_b
            ms_s[:, cs] = m.astype(BF16)
        os_ref[...] = hs + _dot(ms_s[...], wout_s[...])


def _mix(h, hs, s0, s1, norm_g, w_in, b_in, ln_g, ln_b, w_s, b_s, conv_w, w_out, *, layer, seq):
    rows, srows = h.shape[0], hs.shape[0]
    tm = ROW_TILE
    tiles_per_seq = seq // tm
    row_spec = pl.BlockSpec((tm, D_MODEL), lambda i: (i, 0))
    sample = _resident((srows, D_MODEL))
    sample_out = _whole_out((srows, D_MODEL))
    return pl.pallas_call(
        functools.partial(_mix_kernel, layer=layer, tm=tm, tiles_per_seq=tiles_per_seq),
        grid=(rows // tm,),
        in_specs=[row_spec, sample, sample, sample, _resident((1, D_MODEL)), _HBM, _resident((1, IN_COLS)),
                  _resident((1, E_A)), _resident((1, E_A)), _resident((G_A, CHUNK, CHUNK)),
                  _resident((G_A, CHUNK, 1)), _resident((CONV_W, E_B)), _HBM],
        out_specs=[row_spec, pl.BlockSpec((1, CONV_W - 1, E_B), lambda i: (i // tiles_per_seq, 0, 0)),
                   sample_out, sample_out, sample_out],
        out_shape=[jax.ShapeDtypeStruct((rows, D_MODEL), F32),
                   jax.ShapeDtypeStruct((rows // seq, CONV_W - 1, E_B), F32)]
                  + [jax.ShapeDtypeStruct((srows, D_MODEL), F32)] * 3,
        scratch_shapes=[pltpu.VMEM((D_MODEL, IN_COLS), BF16), pltpu.VMEM((D_MODEL, D_MODEL), BF16),
                        pltpu.VMEM((2, ROWS_W_IN, IN_COLS), F32), pltpu.VMEM((2, ROWS_W_OUT, D_MODEL), F32),
                        pltpu.SemaphoreType.DMA((2,)),
                        pltpu.VMEM((tm, E_A), BF16), pltpu.VMEM((HALO + tm, E_B), F32),
                        pltpu.VMEM((tm, E_B), F32), pltpu.VMEM((tm, D_MODEL), BF16),
                        pltpu.VMEM((srows, D_MODEL), BF16)],
        compiler_params=_params(),
        name="mix",
    )(h, hs, s0, s1, norm_g, w_in, b_in, ln_g, ln_b, w_s, b_s, conv_w, w_out)


def kernel(x_prompt, x_sample, state_conv, ffn1_norm, ffn1_w_gate, ffn1_w_up, ffn1_w_down, mix_norm, w_in, b_in,
           v_ln_gain, v_ln_bias, w_spatial, b_spatial, conv_w, w_out, ffn2_norm, ffn2_w_gate, ffn2_w_up,
           ffn2_w_down, final_norm):
    batch, seq, _ = x_prompt.shape
    dec_batch, dec_seq, _ = x_sample.shape
    depth = w_in.shape[0]
    assert dec_seq == 1 and seq % ROW_TILE == 0 and ROW_TILE % CHUNK == 0 and (batch * seq) % FFN_ROW_TILE == 0

    hp = x_prompt.reshape(batch * seq, D_MODEL)
    hs = x_sample.reshape(dec_batch, D_MODEL)
    fin_g = final_norm.reshape(1, D_MODEL)
    row = lambda a: a.reshape(1, -1)
    conv_p, conv_s, v_s = [], [], []
    for l in range(depth):
        last = l == depth - 1
        hp, hs = _ffn(hp, hs, row(ffn1_norm[l]), ffn1_w_gate, ffn1_w_up, ffn1_w_down, fin_g, layer=l, final=False)
        hp, buf_p, hs, xg_s, vn_s = _mix(
            hp, hs, state_conv[l, :, 0], state_conv[l, :, 1], row(mix_norm[l]), w_in, row(b_in[l]),
            row(v_ln_gain[l]), row(v_ln_bias[l]), w_spatial[l], b_spatial[l][:, :, None], conv_w[l], w_out,
            layer=l, seq=seq)
        hp, hs = _ffn(hp, hs, row(ffn2_norm[l]), ffn2_w_gate, ffn2_w_up, ffn2_w_down, fin_g, layer=l, final=last)

        conv_p.append(buf_p)
        conv_s.append(jnp.stack([state_conv[l, :, 1], xg_s], axis=1))
        v_s.append(vn_s.reshape(dec_batch, dec_seq, E_A))

    return (hp.reshape(batch, seq, D_MODEL), hs.reshape(dec_batch, dec_seq, D_MODEL),
            jnp.stack(conv_p, axis=0), jnp.stack(conv_s, axis=0), jnp.stack(v_s, axis=0))
```

```python
import functools

import jax
import jax.numpy as jnp
import numpy as np
from jax import lax
from jax.experimental import pallas as pl
from jax.experimental.pallas import tpu as pltpu

D_MODEL = 1024
E_A = D_MODEL
G_A = 4
HD_A = E_A // G_A
CHUNK = 128
E_B = D_MODEL
CONV_W = 3
D_FF = 2816
EPS = 1e-6
SQRT_HALF = float(np.sqrt(0.5).astype(np.float32))
IN_COLS = 2 * E_A + 3 * E_B + 2 * D_MODEL
COL_U, COL_V, COL_B, COL_C, COL_X, COL_GA, COL_GB = (k * D_MODEL for k in range(7))

ROW_TILE = 512
FF_CHUNK = 256
COL_BLOCK = HD_A
HALO = 8
FFN_SLOTS = 2
MIX_SLOTS = 4
ROWS_W_IN = 64
ROWS_W_OUT = 256
VMEM_LIMIT_BYTES = 56 * 1024 * 1024

F32 = jnp.float32
BF16 = jnp.bfloat16


def _rms_norm(x, g):
    return (x * lax.rsqrt(jnp.mean(x * x, axis=-1, keepdims=True) + EPS)) * g


def _layer_norm(x, g, b):
    mu = jnp.mean(x, axis=-1, keepdims=True)
    xc = x - mu
    return (xc * lax.rsqrt(jnp.mean(xc * xc, axis=-1, keepdims=True) + EPS)) * g + b


def _gelu(x):
    return 0.5 * x * (1.0 + lax.erf(x * SQRT_HALF))


def _dot(a, b):
    return jnp.dot(a, b, preferred_element_type=F32)


def _resident(shape):
    return pl.BlockSpec(shape, lambda i: (0,) * len(shape), pipeline_mode=pl.Buffered(1))


def _layer_resident(shape, layer):
    return pl.BlockSpec((None,) + shape, lambda i: (layer,) + (0,) * len(shape), pipeline_mode=pl.Buffered(1))


def _whole_out(shape):
    return pl.BlockSpec(shape, lambda i: (0,) * len(shape))


_HBM = pl.BlockSpec(memory_space=pl.ANY)


def _params():
    return pltpu.CompilerParams(dimension_semantics=("arbitrary",), vmem_limit_bytes=VMEM_LIMIT_BYTES)


def _no_weights_needed(chunk):
    del chunk


def _stream_cast(src, dst, stage, sem):
    slots, rows = stage.shape[0], stage.shape[1]
    n = src.shape[0] // rows
    ahead = slots - 1
    assert n * rows == src.shape[0] and n >= ahead >= 1

    def copy(k):
        slot = k % slots if isinstance(k, int) else lax.rem(k, slots)
        return pltpu.make_async_copy(src.at[pl.ds(k * rows, rows), :], stage.at[slot], sem.at[slot])

    for k in range(ahead):
        copy(k).start()

    def body(k, carry):
        copy(k).wait()

        @pl.when(k + ahead < n)
        def _():
            copy(k + ahead).start()

        dst[pl.ds(pl.multiple_of(k * rows, rows), rows), :] = stage[lax.rem(k, slots)].astype(BF16)
        return carry

    lax.fori_loop(0, n, body, 0)


def _ffn_kernel(x_ref, xs_ref, g_ref, wg_hbm, wu_hbm, wd_hbm, fg_ref, o_ref, os_ref,
                wg_s, wu_s, wd_s, stage_in, stage_out, sem, *, layer, final):
    step = pl.program_id(0)
    n_chunks = D_FF // FF_CHUNK

    def copies(c):
        slot = c % FFN_SLOTS
        cols = pl.ds(c * FF_CHUNK, FF_CHUNK)
        return (pltpu.make_async_copy(wg_hbm.at[layer, :, cols], stage_in.at[slot, 0], sem.at[slot, 0]),
                pltpu.make_async_copy(wu_hbm.at[layer, :, cols], stage_in.at[slot, 1], sem.at[slot, 1]),
                pltpu.make_async_copy(wd_hbm.at[layer, cols, :], stage_out.at[slot], sem.at[slot, 2]))

    def fetch(c):
        for cp in copies(c):
            cp.start()

    def land(c):
        slot = c % FFN_SLOTS
        sl = slice(c * FF_CHUNK, (c + 1) * FF_CHUNK)
        for cp in copies(c):
            cp.wait()
        wg_s[:, sl] = stage_in[slot, 0].astype(BF16)
        wu_s[:, sl] = stage_in[slot, 1].astype(BF16)
        wd_s[sl, :] = stage_out[slot].astype(BF16)
        if c + FFN_SLOTS < n_chunks:
            fetch(c + FFN_SLOTS)

    def half_step(x, need):
        xn = _rms_norm(x, g_ref[...]).astype(BF16)
        acc = None
        for c in range(n_chunks):
            need(c)
            sl = slice(c * FF_CHUNK, (c + 1) * FF_CHUNK)
            gate = _dot(xn, wg_s[:, sl])
            up = _dot(xn, wu_s[:, sl])
            a = (jax.nn.silu(gate) * up).astype(BF16)
            d = _dot(a, wd_s[sl, :])
            acc = d if acc is None else acc + d
        h = x + 0.5 * acc
        if final:
            h = _rms_norm(h, fg_ref[...])
        return h

    @pl.when(step == 0)
    def _():
        for c in range(FFN_SLOTS):
            fetch(c)
        o_ref[...] = half_step(x_ref[...], land)

    @pl.when(step != 0)
    def _():
        o_ref[...] = half_step(x_ref[...], _no_weights_needed)

    @pl.when(step == pl.num_programs(0) - 1)
    def _():
        os_ref[...] = half_step(xs_ref[...], _no_weights_needed)


def _ffn(x, xs, norm_g, wg, wu, wd, final_g, *, layer, final):
    rows, srows = x.shape[0], xs.shape[0]
    assert rows // ROW_TILE > 1
    row_spec = pl.BlockSpec((ROW_TILE, D_MODEL), lambda i: (i, 0))
    return pl.pallas_call(
        functools.partial(_ffn_kernel, layer=layer, final=final),
        grid=(rows // ROW_TILE,),
        in_specs=[row_spec, _resident((srows, D_MODEL)), _layer_resident((1, D_MODEL), layer), _HBM, _HBM, _HBM,
                  _resident((1, D_MODEL))],
        out_specs=[row_spec, _whole_out((srows, D_MODEL))],
        out_shape=[jax.ShapeDtypeStruct((rows, D_MODEL), F32), jax.ShapeDtypeStruct((srows, D_MODEL), F32)],
        scratch_shapes=[pltpu.VMEM((D_MODEL, D_FF), BF16), pltpu.VMEM((D_MODEL, D_FF), BF16),
                        pltpu.VMEM((D_FF, D_MODEL), BF16),
                        pltpu.VMEM((FFN_SLOTS, 2, D_MODEL, FF_CHUNK), F32),
                        pltpu.VMEM((FFN_SLOTS, FF_CHUNK, D_MODEL), F32),
                        pltpu.SemaphoreType.DMA((FFN_SLOTS, 3))],
        compiler_params=_params(),
        name="ffn_final" if final else "ffn",
    )(x, xs, norm_g, wg, wu, wd, final_g)


def _causal_spatial(ws_ref, g):
    t = lax.broadcasted_iota(jnp.int32, (CHUNK, CHUNK), 0)
    s = lax.broadcasted_iota(jnp.int32, (CHUNK, CHUNK), 1)
    return jnp.where(s <= t, ws_ref[g], 0.0).astype(BF16)


def _mix_kernel(h_ref, hs_ref, st_ref, ng_ref, win_hbm, bin_ref, lng_ref, lnb_ref, ws_ref, bs_ref, cw_ref,
                wout_hbm, o_ref, buf_ref, os_ref, sts_ref, vs_ref,
                win_s, wout_s, stage_in, stage_out, sem, vn_s, xg_s, yb_s, m_s, ms_s, *, layer, tm, tiles_per_seq):
    step = pl.program_id(0)

    @pl.when(step == 0)
    def _():
        _stream_cast(win_hbm.at[layer], win_s, stage_in, sem)
        _stream_cast(wout_hbm.at[layer], wout_s, stage_out, sem)

    @pl.when(step % tiles_per_seq == 0)
    def _():
        xg_s[0:HALO, :] = jnp.zeros((HALO, E_B), F32)

    h = h_ref[...]
    n = _rms_norm(h, ng_ref[...]).astype(BF16)

    def proj(col, width=COL_BLOCK):
        return _dot(n, win_s[:, col:col + width]) + bin_ref[:, col:col + width]

    vs = []
    for g in range(G_A):
        c0 = g * COL_BLOCK
        cs = slice(c0, c0 + COL_BLOCK)
        vs.append(_gelu(proj(COL_V + c0)))
        gate_b = proj(COL_B + c0)
        xg = proj(COL_C + c0) * proj(COL_X + c0)
        xg_s[HALO:HALO + tm, cs] = xg
        conv = cw_ref[0:1, cs] * xg_s[HALO - 2:HALO - 2 + tm, cs]
        conv = conv + cw_ref[1:2, cs] * xg_s[HALO - 1:HALO - 1 + tm, cs]
        conv = conv + cw_ref[2:3, cs] * xg
        y_b = gate_b * conv
        yb_s[:, cs] = jax.nn.sigmoid(proj(COL_GB + c0)) * y_b

    mu = sum(jnp.sum(v, axis=-1, keepdims=True) for v in vs) / E_A
    vs = [v - mu for v in vs]
    var = sum(jnp.sum(v * v, axis=-1, keepdims=True) for v in vs) / E_A
    rstd = lax.rsqrt(var + EPS)
    for g in range(G_A):
        cs = slice(g * COL_BLOCK, (g + 1) * COL_BLOCK)
        vn_s[:, cs] = ((vs[g] * rstd) * lng_ref[:, cs] + lnb_ref[:, cs]).astype(BF16)

    for g in range(G_A):
        c0 = g * COL_BLOCK
        cs = slice(c0, c0 + COL_BLOCK)
        u = _gelu(proj(COL_U + c0))
        w_mix = _causal_spatial(ws_ref, g)
        z = jnp.concatenate(
            [_dot(w_mix, vn_s[c * CHUNK:(c + 1) * CHUNK, cs]) + bs_ref[g] for c in range(tm // CHUNK)], axis=0)
        y_a = u * z
        m = jax.nn.sigmoid(proj(COL_GA + c0)) * y_a + yb_s[:, cs]
        m_s[:, cs] = m.astype(BF16)

    last = xg_s[HALO + tm - 2:HALO + tm, :]
    buf_ref[0] = last
    xg_s[HALO - 2:HALO, :] = last
    o_ref[...] = h + _dot(m_s[...], wout_s[...])

    @pl.when(step == pl.num_programs(0) - 1)
    def _():
        hs = hs_ref[...]
        ns = _rms_norm(hs, ng_ref[...]).astype(BF16)

        def proj_s(col, width=COL_BLOCK):
            return _dot(ns, win_s[:, col:col + width]) + bin_ref[:, col:col + width]

        v = _layer_norm(_gelu(proj_s(COL_V, E_A)), lng_ref[...], lnb_ref[...])
        vs_ref[...] = v
        for g in range(G_A):
            c0 = g * COL_BLOCK
            cs = slice(c0, c0 + COL_BLOCK)
            u = _gelu(proj_s(COL_U + c0))
            z = ws_ref[g][0:1, 0:1] * v[:, cs] + bs_ref[g][0:1, :]
            y_a = u * z
            gate_b = proj_s(COL_B + c0)
            xg = proj_s(COL_C + c0) * proj_s(COL_X + c0)
            older, newer = st_ref[:, cs], st_ref[:, E_B + c0:E_B + c0 + COL_BLOCK]
            sts_ref[:, cs] = newer
            sts_ref[:, E_B + c0:E_B + c0 + COL_BLOCK] = xg
            conv = cw_ref[0:1, cs] * older
            conv = conv + cw_ref[1:2, cs] * newer
            conv = conv + cw_ref[2:3, cs] * xg
            y_b = gate_b * conv
            m = jax.nn.sigmoid(proj_s(COL_GA + c0)) * y_a + jax.nn.sigmoid(proj_s(COL_GB + c0)) * y_b
            ms_s[:, cs] = m.astype(BF16)
        os_ref[...] = hs + _dot(ms_s[...], wout_s[...])


def _mix(h, hs, state, norm_g, w_in, b_in, ln_g, ln_b, w_s, b_s, conv_w, w_out, *, layer, seq):
    rows, srows = h.shape[0], hs.shape[0]
    tm = ROW_TILE
    tiles_per_seq = seq // tm
    row_spec = pl.BlockSpec((tm, D_MODEL), lambda i: (i, 0))
    sample = _resident((srows, D_MODEL))
    sample_out = _whole_out((srows, D_MODEL))
    return pl.pallas_call(
        functools.partial(_mix_kernel, layer=layer, tm=tm, tiles_per_seq=tiles_per_seq),
        grid=(rows // tm,),
        in_specs=[row_spec, sample, _layer_resident((srows, (CONV_W - 1) * E_B), layer),
                  _layer_resident((1, D_MODEL), layer), _HBM, _layer_resident((1, IN_COLS), layer),
                  _layer_resident((1, E_A), layer), _layer_resident((1, E_A), layer),
                  _layer_resident((G_A, CHUNK, CHUNK), layer), _resident((G_A, CHUNK, 1)),
                  _layer_resident((CONV_W, E_B), layer), _HBM],
        out_specs=[row_spec, pl.BlockSpec((1, CONV_W - 1, E_B), lambda i: (i // tiles_per_seq, 0, 0)),
                   sample_out, _whole_out((srows, (CONV_W - 1) * E_B)), sample_out],
        out_shape=[jax.ShapeDtypeStruct((rows, D_MODEL), F32),
                   jax.ShapeDtypeStruct((rows // seq, CONV_W - 1, E_B), F32),
                   jax.ShapeDtypeStruct((srows, D_MODEL), F32),
                   jax.ShapeDtypeStruct((srows, (CONV_W - 1) * E_B), F32),
                   jax.ShapeDtypeStruct((srows, E_A), F32)],
        scratch_shapes=[pltpu.VMEM((D_MODEL, IN_COLS), BF16), pltpu.VMEM((D_MODEL, D_MODEL), BF16),
                        pltpu.VMEM((MIX_SLOTS, ROWS_W_IN, IN_COLS), F32),
                        pltpu.VMEM((MIX_SLOTS, ROWS_W_OUT, D_MODEL), F32),
                        pltpu.SemaphoreType.DMA((MIX_SLOTS,)),
                        pltpu.VMEM((tm, E_A), BF16), pltpu.VMEM((HALO + tm, E_B), F32),
                        pltpu.VMEM((tm, E_B), F32), pltpu.VMEM((tm, D_MODEL), BF16),
                        pltpu.VMEM((srows, D_MODEL), BF16)],
        compiler_params=_params(),
        name="mix",
    )(h, hs, state, norm_g, w_in, b_in, ln_g, ln_b, w_s, b_s, conv_w, w_out)


def kernel(x_prompt, x_sample, state_conv, ffn1_norm, ffn1_w_gate, ffn1_w_up, ffn1_w_down, mix_norm, w_in, b_in,
           v_ln_gain, v_ln_bias, w_spatial, b_spatial, conv_w, w_out, ffn2_norm, ffn2_w_gate, ffn2_w_up,
           ffn2_w_down, final_norm):
    batch, seq, _ = x_prompt.shape
    dec_batch, dec_seq, _ = x_sample.shape
    depth = w_in.shape[0]
    assert dec_seq == 1 and seq % ROW_TILE == 0 and ROW_TILE % CHUNK == 0

    hp = x_prompt.reshape(batch * seq, D_MODEL)
    hs = x_sample.reshape(dec_batch, D_MODEL)
    fin_g = final_norm.reshape(1, D_MODEL)
    rows_of = lambda a: a.reshape(depth, 1, -1)
    state = state_conv.reshape(depth, dec_batch, (CONV_W - 1) * E_B)
    conv_p, conv_s, v_s = [], [], []
    for l in range(depth):
        last = l == depth - 1
        hp, hs = _ffn(hp, hs, rows_of(ffn1_norm), ffn1_w_gate, ffn1_w_up, ffn1_w_down, fin_g, layer=l, final=False)
        hp, buf_p, hs, state_s, vn_s = _mix(
            hp, hs, state, rows_of(mix_norm), w_in, rows_of(b_in), rows_of(v_ln_gain), rows_of(v_ln_bias),
            w_spatial, b_spatial[l][:, :, None], conv_w, w_out, layer=l, seq=seq)
        hp, hs = _ffn(hp, hs, rows_of(ffn2_norm), ffn2_w_gate, ffn2_w_up, ffn2_w_down, fin_g, layer=l, final=last)

        conv_p.append(buf_p)
        conv_s.append(state_s.reshape(dec_batch, CONV_W - 1, E_B))
        v_s.append(vn_s.reshape(dec_batch, dec_seq, E_A))

    return (hp.reshape(batch, seq, D_MODEL), hs.reshape(dec_batch, dec_seq, D_MODEL),
            jnp.stack(conv_p, axis=0), jnp.stack(conv_s, axis=0), jnp.stack(v_s, axis=0))
```

```python
import functools

import jax
import jax.numpy as jnp
import numpy as np
from jax import lax
from jax.experimental import pallas as pl
from jax.experimental.pallas import tpu as pltpu

D_MODEL = 1024
E_A = D_MODEL
G_A = 4
HD_A = E_A // G_A
CHUNK = 128
E_B = D_MODEL
CONV_W = 3
D_FF = 2816
EPS = 1e-6
SQRT_HALF = float(np.sqrt(0.5).astype(np.float32))
IN_COLS = 2 * E_A + 3 * E_B + 2 * D_MODEL
COL_U, COL_V, COL_B, COL_C, COL_X, COL_GA, COL_GB = (k * D_MODEL for k in range(7))

ROW_TILE = 512
FF_CHUNK = 256
COL_BLOCK = HD_A
HALO = 8
FFN_SLOTS = 3
MIX_SLOTS = 3
ROWS_W_IN = 128
ROWS_W_OUT = 256
VMEM_LIMIT_BYTES = 56 * 1024 * 1024

F32 = jnp.float32
BF16 = jnp.bfloat16


def _rms_norm(x, g):
    return (x * lax.rsqrt(jnp.mean(x * x, axis=-1, keepdims=True) + EPS)) * g


def _layer_norm(x, g, b):
    mu = jnp.mean(x, axis=-1, keepdims=True)
    xc = x - mu
    return (xc * lax.rsqrt(jnp.mean(xc * xc, axis=-1, keepdims=True) + EPS)) * g + b


def _gelu(x):
    return 0.5 * x * (1.0 + lax.erf(x * SQRT_HALF))


def _dot(a, b):
    return jnp.dot(a, b, preferred_element_type=F32)


def _resident(shape):
    return pl.BlockSpec(shape, lambda i: (0,) * len(shape), pipeline_mode=pl.Buffered(1))


def _layer_resident(shape, layer):
    return pl.BlockSpec((None,) + shape, lambda i: (layer,) + (0,) * len(shape), pipeline_mode=pl.Buffered(1))


def _whole_out(shape):
    return pl.BlockSpec(shape, lambda i: (0,) * len(shape))


_HBM = pl.BlockSpec(memory_space=pl.ANY)


def _params():
    return pltpu.CompilerParams(dimension_semantics=("arbitrary",), vmem_limit_bytes=VMEM_LIMIT_BYTES)


def _no_weights_needed(chunk):
    del chunk


def _stream_cast(src, dst, stage, sem):
    slots, rows = stage.shape[0], stage.shape[1]
    n = src.shape[0] // rows
    ahead = slots - 1
    assert n * rows == src.shape[0] and n >= ahead >= 1

    def copy(k):
        slot = k % slots if isinstance(k, int) else lax.rem(k, slots)
        return pltpu.make_async_copy(src.at[pl.ds(k * rows, rows), :], stage.at[slot], sem.at[slot])

    for k in range(ahead):
        copy(k).start()

    def body(k, carry):
        copy(k).wait()

        @pl.when(k + ahead < n)
        def _():
            copy(k + ahead).start()

        dst[pl.ds(pl.multiple_of(k * rows, rows), rows), :] = stage[lax.rem(k, slots)].astype(BF16)
        return carry

    lax.fori_loop(0, n, body, 0)


def _ffn_kernel(x_ref, xs_ref, g_ref, wg_hbm, wu_hbm, wd_hbm, fg_ref, o_ref, os_ref,
                wg_s, wu_s, wd_s, stage_in, stage_out, sem, *, layer, final):
    step = pl.program_id(0)
    n_chunks = D_FF // FF_CHUNK

    def copies(c):
        slot = c % FFN_SLOTS
        cols = pl.ds(c * FF_CHUNK, FF_CHUNK)
        return (pltpu.make_async_copy(wg_hbm.at[layer, :, cols], stage_in.at[slot, 0], sem.at[slot, 0]),
                pltpu.make_async_copy(wu_hbm.at[layer, :, cols], stage_in.at[slot, 1], sem.at[slot, 1]),
                pltpu.make_async_copy(wd_hbm.at[layer, cols, :], stage_out.at[slot], sem.at[slot, 2]))

    def fetch(c):
        for cp in copies(c):
            cp.start()

    def land(c):
        slot = c % FFN_SLOTS
        sl = slice(c * FF_CHUNK, (c + 1) * FF_CHUNK)
        for cp in copies(c):
            cp.wait()
        wg_s[:, sl] = stage_in[slot, 0].astype(BF16)
        wu_s[:, sl] = stage_in[slot, 1].astype(BF16)
        wd_s[sl, :] = stage_out[slot].astype(BF16)
        if c + FFN_SLOTS < n_chunks:
            fetch(c + FFN_SLOTS)

    def half_step(x, need):
        xn = _rms_norm(x, g_ref[...]).astype(BF16)
        acc = None
        for c in range(n_chunks):
            need(c)
            sl = slice(c * FF_CHUNK, (c + 1) * FF_CHUNK)
            gate = _dot(xn, wg_s[:, sl])
            up = _dot(xn, wu_s[:, sl])
            a = (jax.nn.silu(gate) * up).astype(BF16)
            d = _dot(a, wd_s[sl, :])
            acc = d if acc is None else acc + d
        h = x + 0.5 * acc
        if final:
            h = _rms_norm(h, fg_ref[...])
        return h

    @pl.when(step == 0)
    def _():
        for c in range(FFN_SLOTS):
            fetch(c)
        o_ref[...] = half_step(x_ref[...], land)

    @pl.when(step != 0)
    def _():
        o_ref[...] = half_step(x_ref[...], _no_weights_needed)

    @pl.when(step == pl.num_programs(0) - 1)
    def _():
        os_ref[...] = half_step(xs_ref[...], _no_weights_needed)


def _ffn(x, xs, norm_g, wg, wu, wd, final_g, *, layer, final):
    rows, srows = x.shape[0], xs.shape[0]
    assert rows // ROW_TILE > 1
    row_spec = pl.BlockSpec((ROW_TILE, D_MODEL), lambda i: (i, 0))
    return pl.pallas_call(
        functools.partial(_ffn_kernel, layer=layer, final=final),
        grid=(rows // ROW_TILE,),
        in_specs=[row_spec, _resident((srows, D_MODEL)), _layer_resident((1, D_MODEL), layer), _HBM, _HBM, _HBM,
                  _resident((1, D_MODEL))],
        out_specs=[row_spec, _whole_out((srows, D_MODEL))],
        out_shape=[jax.ShapeDtypeStruct((rows, D_MODEL), F32), jax.ShapeDtypeStruct((srows, D_MODEL), F32)],
        scratch_shapes=[pltpu.VMEM((D_MODEL, D_FF), BF16), pltpu.VMEM((D_MODEL, D_FF), BF16),
                        pltpu.VMEM((D_FF, D_MODEL), BF16),
                        pltpu.VMEM((FFN_SLOTS, 2, D_MODEL, FF_CHUNK), F32),
                        pltpu.VMEM((FFN_SLOTS, FF_CHUNK, D_MODEL), F32),
                        pltpu.SemaphoreType.DMA((FFN_SLOTS, 3))],
        compiler_params=_params(),
        name="ffn_final" if final else "ffn",
    )(x, xs, norm_g, wg, wu, wd, final_g)


def _causal_spatial(ws_ref, g):
    t = lax.broadcasted_iota(jnp.int32, (CHUNK, CHUNK), 0)
    s = lax.broadcasted_iota(jnp.int32, (CHUNK, CHUNK), 1)
    return jnp.where(s <= t, ws_ref[g], 0.0).astype(BF16)


def _mix_kernel(h_ref, hs_ref, st_ref, ng_ref, win_hbm, bin_ref, lng_ref, lnb_ref, ws_ref, bs_ref, cw_ref,
                wout_hbm, o_ref, buf_ref, os_ref, sts_ref, vs_ref,
                win_s, wout_s, stage_in, stage_out, sem, vn_s, xg_s, yb_s, m_s, ms_s, *, layer, tm, tiles_per_seq):
    step = pl.program_id(0)

    @pl.when(step == 0)
    def _():
        _stream_cast(win_hbm.at[layer], win_s, stage_in, sem)
        _stream_cast(wout_hbm.at[layer], wout_s, stage_out, sem)

    @pl.when(step % tiles_per_seq == 0)
    def _():
        xg_s[0:HALO, :] = jnp.zeros((HALO, E_B), F32)

    h = h_ref[...]
    n = _rms_norm(h, ng_ref[...]).astype(BF16)

    def proj(col, width=COL_BLOCK):
        return _dot(n, win_s[:, col:col + width]) + bin_ref[:, col:col + width]

    vs = []
    for g in range(G_A):
        c0 = g * COL_BLOCK
        cs = slice(c0, c0 + COL_BLOCK)
        vs.append(_gelu(proj(COL_V + c0)))
        gate_b = proj(COL_B + c0)
        xg = proj(COL_C + c0) * proj(COL_X + c0)
        xg_s[HALO:HALO + tm, cs] = xg
        conv = cw_ref[0:1, cs] * xg_s[HALO - 2:HALO - 2 + tm, cs]
        conv = conv + cw_ref[1:2, cs] * xg_s[HALO - 1:HALO - 1 + tm, cs]
        conv = conv + cw_ref[2:3, cs] * xg
        y_b = gate_b * conv
        yb_s[:, cs] = jax.nn.sigmoid(proj(COL_GB + c0)) * y_b

    mu = sum(jnp.sum(v, axis=-1, keepdims=True) for v in vs) / E_A
    vs = [v - mu for v in vs]
    var = sum(jnp.sum(v * v, axis=-1, keepdims=True) for v in vs) / E_A
    rstd = lax.rsqrt(var + EPS)
    for g in range(G_A):
        cs = slice(g * COL_BLOCK, (g + 1) * COL_BLOCK)
        vn_s[:, cs] = ((vs[g] * rstd) * lng_ref[:, cs] + lnb_ref[:, cs]).astype(BF16)

    for g in range(G_A):
        c0 = g * COL_BLOCK
        cs = slice(c0, c0 + COL_BLOCK)
        u = _gelu(proj(COL_U + c0))
        w_mix = _causal_spatial(ws_ref, g)
        z = jnp.concatenate(
            [_dot(w_mix, vn_s[c * CHUNK:(c + 1) * CHUNK, cs]) + bs_ref[g] for c in range(tm // CHUNK)], axis=0)
        y_a = u * z
        m = jax.nn.sigmoid(proj(COL_GA + c0)) * y_a + yb_s[:, cs]
        m_s[:, cs] = m.astype(BF16)

    last = xg_s[HALO + tm - 2:HALO + tm, :]
    buf_ref[0] = last
    xg_s[HALO - 2:HALO, :] = last
    o_ref[...] = h + _dot(m_s[...], wout_s[...])

    @pl.when(step == pl.num_programs(0) - 1)
    def _():
        hs = hs_ref[...]
        ns = _rms_norm(hs, ng_ref[...]).astype(BF16)

        def proj_s(col, width=COL_BLOCK):
            return _dot(ns, win_s[:, col:col + width]) + bin_ref[:, col:col + width]

        v = _layer_norm(_gelu(proj_s(COL_V, E_A)), lng_ref[...], lnb_ref[...])
        vs_ref[...] = v
        for g in range(G_A):
            c0 = g * COL_BLOCK
            cs = slice(c0, c0 + COL_BLOCK)
            u = _gelu(proj_s(COL_U + c0))
            z = ws_ref[g][0:1, 0:1] * v[:, cs] + bs_ref[g][0:1, :]
            y_a = u * z
            gate_b = proj_s(COL_B + c0)
            xg = proj_s(COL_C + c0) * proj_s(COL_X + c0)
            older, newer = st_ref[:, cs], st_ref[:, E_B + c0:E_B + c0 + COL_BLOCK]
            sts_ref[:, cs] = newer
            sts_ref[:, E_B + c0:E_B + c0 + COL_BLOCK] = xg
            conv = cw_ref[0:1, cs] * older
            conv = conv + cw_ref[1:2, cs] * newer
            conv = conv + cw_ref[2:3, cs] * xg
            y_b = gate_b * conv
            m = jax.nn.sigmoid(proj_s(COL_GA + c0)) * y_a + jax.nn.sigmoid(proj_s(COL_GB + c0)) * y_b
            ms_s[:, cs] = m.astype(BF16)
        os_ref[...] = hs + _dot(ms_s[...], wout_s[...])


def _mix(h, hs, state, norm_g, w_in, b_in, ln_g, ln_b, w_s, b_s, conv_w, w_out, *, layer, seq):
    rows, srows = h.shape[0], hs.shape[0]
    tm = ROW_TILE
    tiles_per_seq = seq // tm
    row_spec = pl.BlockSpec((tm, D_MODEL), lambda i: (i, 0))
    sample = _resident((srows, D_MODEL))
    sample_out = _whole_out((srows, D_MODEL))
    return pl.pallas_call(
        functools.partial(_mix_kernel, layer=layer, tm=tm, tiles_per_seq=tiles_per_seq),
        grid=(rows // tm,),
        in_specs=[row_spec, sample, _layer_resident((srows, (CONV_W - 1) * E_B), layer),
                  _layer_resident((1, D_MODEL), layer), _HBM, _layer_resident((1, IN_COLS), layer),
                  _layer_resident((1, E_A), layer), _layer_resident((1, E_A), layer),
                  _layer_resident((G_A, CHUNK, CHUNK), layer), _resident((G_A, CHUNK, 1)),
                  _layer_resident((CONV_W, E_B), layer), _HBM],
        out_specs=[row_spec, pl.BlockSpec((1, CONV_W - 1, E_B), lambda i: (i // tiles_per_seq, 0, 0)),
                   sample_out, _whole_out((srows, (CONV_W - 1) * E_B)), sample_out],
        out_shape=[jax.ShapeDtypeStruct((rows, D_MODEL), F32),
                   jax.ShapeDtypeStruct((rows // seq, CONV_W - 1, E_B), F32),
                   jax.ShapeDtypeStruct((srows, D_MODEL), F32),
                   jax.ShapeDtypeStruct((srows, (CONV_W - 1) * E_B), F32),
                   jax.ShapeDtypeStruct((srows, E_A), F32)],
        scratch_shapes=[pltpu.VMEM((D_MODEL, IN_COLS), BF16), pltpu.VMEM((D_MODEL, D_MODEL), BF16),
                        pltpu.VMEM((MIX_SLOTS, ROWS_W_IN, IN_COLS), F32),
                        pltpu.VMEM((MIX_SLOTS, ROWS_W_OUT, D_MODEL), F32),
                        pltpu.SemaphoreType.DMA((MIX_SLOTS,)),
                        pltpu.VMEM((tm, E_A), BF16), pltpu.VMEM((HALO + tm, E_B), F32),
                        pltpu.VMEM((tm, E_B), F32), pltpu.VMEM((tm, D_MODEL), BF16),
                        pltpu.VMEM((srows, D_MODEL), BF16)],
        compiler_params=_params(),
        name="mix",
    )(h, hs, state, norm_g, w_in, b_in, ln_g, ln_b, w_s, b_s, conv_w, w_out)


def kernel(x_prompt, x_sample, state_conv, ffn1_norm, ffn1_w_gate, ffn1_w_up, ffn1_w_down, mix_norm, w_in, b_in,
           v_ln_gain, v_ln_bias, w_spatial, b_spatial, conv_w, w_out, ffn2_norm, ffn2_w_gate, ffn2_w_up,
           ffn2_w_down, final_norm):
    batch, seq, _ = x_prompt.shape
    dec_batch, dec_seq, _ = x_sample.shape
    depth = w_in.shape[0]
    assert dec_seq == 1 and seq % ROW_TILE == 0 and ROW_TILE % CHUNK == 0

    hp = x_prompt.reshape(batch * seq, D_MODEL)
    hs = x_sample.reshape(dec_batch, D_MODEL)
    fin_g = final_norm.reshape(1, D_MODEL)
    rows_of = lambda a: a.reshape(depth, 1, -1)
    state = state_conv.reshape(depth, dec_batch, (CONV_W - 1) * E_B)
    conv_p, conv_s, v_s = [], [], []
    for l in range(depth):
        last = l == depth - 1
        hp, hs = _ffn(hp, hs, rows_of(ffn1_norm), ffn1_w_gate, ffn1_w_up, ffn1_w_down, fin_g, layer=l, final=False)
        hp, buf_p, hs, state_s, vn_s = _mix(
            hp, hs, state, rows_of(mix_norm), w_in, rows_of(b_in), rows_of(v_ln_gain), rows_of(v_ln_bias),
            w_spatial, b_spatial[l][:, :, None], conv_w, w_out, layer=l, seq=seq)
        hp, hs = _ffn(hp, hs, rows_of(ffn2_norm), ffn2_w_gate, ffn2_w_up, ffn2_w_down, fin_g, layer=l, final=last)

        conv_p.append(buf_p)
        conv_s.append(state_s.reshape(dec_batch, CONV_W - 1, E_B))
        v_s.append(vn_s.reshape(dec_batch, dec_seq, E_A))

    return (hp.reshape(batch, seq, D_MODEL), hs.reshape(dec_batch, dec_seq, D_MODEL),
            jnp.stack(conv_p, axis=0), jnp.stack(conv_s, axis=0), jnp.stack(v_s, axis=0))
```

```python
import functools

import jax
import jax.numpy as jnp
import numpy as np
from jax import lax
from jax.experimental import pallas as pl
from jax.experimental.pallas import tpu as pltpu

D_MODEL = 1024
E_A = D_MODEL
G_A = 4
HD_A = E_A // G_A
CHUNK = 128
E_B = D_MODEL
CONV_W = 3
D_FF = 2816
EPS = 1e-6
SQRT_HALF = float(np.sqrt(0.5).astype(np.float32))
IN_COLS = 2 * E_A + 3 * E_B + 2 * D_MODEL
COL_U, COL_V, COL_B, COL_C, COL_X, COL_GA, COL_GB = (k * D_MODEL for k in range(7))

ROW_TILE = 512
FF_CHUNK = 256
COL_BLOCK = HD_A
HALO = 8
FFN_SLOTS = 3
MIX_SLOTS = 2
VMEM_LIMIT_BYTES = 56 * 1024 * 1024

F32 = jnp.float32
BF16 = jnp.bfloat16


def _rms_norm(x, g):
    return (x * lax.rsqrt(jnp.mean(x * x, axis=-1, keepdims=True) + EPS)) * g


def _layer_norm(x, g, b):
    mu = jnp.mean(x, axis=-1, keepdims=True)
    xc = x - mu
    return (xc * lax.rsqrt(jnp.mean(xc * xc, axis=-1, keepdims=True) + EPS)) * g + b


def _gelu(x):
    return 0.5 * x * (1.0 + lax.erf(x * SQRT_HALF))


def _dot(a, b):
    return jnp.dot(a, b, preferred_element_type=F32)


def _resident(shape):
    return pl.BlockSpec(shape, lambda i: (0,) * len(shape), pipeline_mode=pl.Buffered(1))


def _layer_resident(shape, layer):
    return pl.BlockSpec((None,) + shape, lambda i: (layer,) + (0,) * len(shape), pipeline_mode=pl.Buffered(1))


def _whole_out(shape):
    return pl.BlockSpec(shape, lambda i: (0,) * len(shape))


_HBM = pl.BlockSpec(memory_space=pl.ANY)


def _params():
    return pltpu.CompilerParams(dimension_semantics=("arbitrary",), vmem_limit_bytes=VMEM_LIMIT_BYTES)


def _no_weights_needed(chunk):
    del chunk


def _ffn_kernel(x_ref, xs_ref, g_ref, wg_hbm, wu_hbm, wd_hbm, fg_ref, o_ref, os_ref,
                wg_s, wu_s, wd_s, stage_in, stage_out, sem, *, layer, final):
    step = pl.program_id(0)
    n_chunks = D_FF // FF_CHUNK

    def copies(c):
        slot = c % FFN_SLOTS
        cols = pl.ds(c * FF_CHUNK, FF_CHUNK)
        return (pltpu.make_async_copy(wg_hbm.at[layer, :, cols], stage_in.at[slot, 0], sem.at[slot, 0]),
                pltpu.make_async_copy(wu_hbm.at[layer, :, cols], stage_in.at[slot, 1], sem.at[slot, 1]),
                pltpu.make_async_copy(wd_hbm.at[layer, cols, :], stage_out.at[slot], sem.at[slot, 2]))

    def fetch(c):
        for cp in copies(c):
            cp.start()

    def land(c):
        slot = c % FFN_SLOTS
        sl = slice(c * FF_CHUNK, (c + 1) * FF_CHUNK)
        for cp in copies(c):
            cp.wait()
        wg_s[:, sl] = stage_in[slot, 0].astype(BF16)
        wu_s[:, sl] = stage_in[slot, 1].astype(BF16)
        wd_s[sl, :] = stage_out[slot].astype(BF16)
        if c + FFN_SLOTS < n_chunks:
            fetch(c + FFN_SLOTS)

    def half_step(x, need):
        xn = _rms_norm(x, g_ref[...]).astype(BF16)
        acc = None
        for c in range(n_chunks):
            need(c)
            sl = slice(c * FF_CHUNK, (c + 1) * FF_CHUNK)
            gate = _dot(xn, wg_s[:, sl])
            up = _dot(xn, wu_s[:, sl])
            a = (jax.nn.silu(gate) * up).astype(BF16)
            d = _dot(a, wd_s[sl, :])
            acc = d if acc is None else acc + d
        h = x + 0.5 * acc
        if final:
            h = _rms_norm(h, fg_ref[...])
        return h

    @pl.when(step == 0)
    def _():
        for c in range(FFN_SLOTS):
            fetch(c)
        o_ref[...] = half_step(x_ref[...], land)

    @pl.when(step != 0)
    def _():
        o_ref[...] = half_step(x_ref[...], _no_weights_needed)

    @pl.when(step == pl.num_programs(0) - 1)
    def _():
        os_ref[...] = half_step(xs_ref[...], _no_weights_needed)


def _ffn(x, xs, norm_g, wg, wu, wd, final_g, *, layer, final):
    rows, srows = x.shape[0], xs.shape[0]
    assert rows // ROW_TILE > 1
    row_spec = pl.BlockSpec((ROW_TILE, D_MODEL), lambda i: (i, 0))
    return pl.pallas_call(
        functools.partial(_ffn_kernel, layer=layer, final=final),
        grid=(rows // ROW_TILE,),
        in_specs=[row_spec, _resident((srows, D_MODEL)), _layer_resident((1, D_MODEL), layer), _HBM, _HBM, _HBM,
                  _resident((1, D_MODEL))],
        out_specs=[row_spec, _whole_out((srows, D_MODEL))],
        out_shape=[jax.ShapeDtypeStruct((rows, D_MODEL), F32), jax.ShapeDtypeStruct((srows, D_MODEL), F32)],
        scratch_shapes=[pltpu.VMEM((D_MODEL, D_FF), BF16), pltpu.VMEM((D_MODEL, D_FF), BF16),
                        pltpu.VMEM((D_FF, D_MODEL), BF16),
                        pltpu.VMEM((FFN_SLOTS, 2, D_MODEL, FF_CHUNK), F32),
                        pltpu.VMEM((FFN_SLOTS, FF_CHUNK, D_MODEL), F32),
                        pltpu.SemaphoreType.DMA((FFN_SLOTS, 3))],
        compiler_params=_params(),
        name="ffn_final" if final else "ffn",
    )(x, xs, norm_g, wg, wu, wd, final_g)


def _causal_spatial(ws_ref, g):
    t = lax.broadcasted_iota(jnp.int32, (CHUNK, CHUNK), 0)
    s = lax.broadcasted_iota(jnp.int32, (CHUNK, CHUNK), 1)
    return jnp.where(s <= t, ws_ref[g], 0.0).astype(BF16)


_MIX_SLABS = (COL_V, COL_B, COL_C, COL_X, COL_GB, COL_U, COL_GA)


def _mix_kernel(h_ref, hs_ref, st_ref, ng_ref, win_hbm, bin_ref, lng_ref, lnb_ref, ws_ref, bs_ref, cw_ref,
                wout_hbm, o_ref, buf_ref, os_ref, sts_ref, vs_ref,
                win_s, wout_s, stage, sem, vn_s, xg_s, yb_s, m_s, ms_s, *, layer, tm, tiles_per_seq):
    step = pl.program_id(0)
    groups = range(G_A)

    def cols(base, g):
        return slice(base + g * COL_BLOCK, base + (g + 1) * COL_BLOCK)

    def copy(k):
        src = wout_hbm.at[layer] if k == len(_MIX_SLABS) else win_hbm.at[layer, :, pl.ds(_MIX_SLABS[k], D_MODEL)]
        return pltpu.make_async_copy(src, stage.at[k % MIX_SLOTS], sem.at[k % MIX_SLOTS])

    def land(k):
        copy(k).wait()
        if k == len(_MIX_SLABS):
            wout_s[...] = stage[k % MIX_SLOTS].astype(BF16)
        else:
            win_s[:, _MIX_SLABS[k]:_MIX_SLABS[k] + D_MODEL] = stage[k % MIX_SLOTS].astype(BF16)
        if k + MIX_SLOTS <= len(_MIX_SLABS):
            copy(k + MIX_SLOTS).start()

    def conv_branch(g, gate_b, xg, gate_gb):
        cs = cols(0, g)
        xg_s[HALO:HALO + tm, cs] = xg
        conv = cw_ref[0:1, cs] * xg_s[HALO - 2:HALO - 2 + tm, cs]
        conv = conv + cw_ref[1:2, cs] * xg_s[HALO - 1:HALO - 1 + tm, cs]
        conv = conv + cw_ref[2:3, cs] * xg
        y_b = gate_b * conv
        yb_s[:, cs] = jax.nn.sigmoid(gate_gb()) * y_b

    def norm_v(vs):
        mu = sum(jnp.sum(v, axis=-1, keepdims=True) for v in vs) / E_A
        vs = [v - mu for v in vs]
        var = sum(jnp.sum(v * v, axis=-1, keepdims=True) for v in vs) / E_A
        rstd = lax.rsqrt(var + EPS)
        for g in groups:
            cs = cols(0, g)
            vn_s[:, cs] = ((vs[g] * rstd) * lng_ref[:, cs] + lnb_ref[:, cs]).astype(BF16)

    def mix_branch(g, u, gate_ga):
        cs = cols(0, g)
        w_mix = _causal_spatial(ws_ref, g)
        z = jnp.concatenate(
            [_dot(w_mix, vn_s[c * CHUNK:(c + 1) * CHUNK, cs]) + bs_ref[g] for c in range(tm // CHUNK)], axis=0)
        y_a = u * z
        m = jax.nn.sigmoid(gate_ga()) * y_a + yb_s[:, cs]
        m_s[:, cs] = m.astype(BF16)

    def finish(h):
        last = xg_s[HALO + tm - 2:HALO + tm, :]
        buf_ref[0] = last
        xg_s[HALO - 2:HALO, :] = last
        o_ref[...] = h + _dot(m_s[...], wout_s[...])

    @pl.when(step % tiles_per_seq == 0)
    def _():
        xg_s[0:HALO, :] = jnp.zeros((HALO, E_B), F32)

    @pl.when(step == 0)
    def _():
        for k in range(MIX_SLOTS):
            copy(k).start()
        h = h_ref[...]
        n = _rms_norm(h, ng_ref[...]).astype(BF16)

        def proj(base, g):
            return _dot(n, win_s[:, cols(base, g)]) + bin_ref[:, cols(base, g)]

        slab = {base: k for k, base in enumerate(_MIX_SLABS)}
        land(slab[COL_V])
        vs = [_gelu(proj(COL_V, g)) for g in groups]
        land(slab[COL_B])
        gate_b = [proj(COL_B, g) for g in groups]
        land(slab[COL_C])
        xg = [proj(COL_C, g) for g in groups]
        land(slab[COL_X])
        xg = [xg[g] * proj(COL_X, g) for g in groups]
        land(slab[COL_GB])
        for g in groups:
            conv_branch(g, gate_b[g], xg[g], functools.partial(proj, COL_GB, g))
        norm_v(vs)
        land(slab[COL_U])
        us = [_gelu(proj(COL_U, g)) for g in groups]
        land(slab[COL_GA])
        for g in groups:
            mix_branch(g, us[g], functools.partial(proj, COL_GA, g))
        land(len(_MIX_SLABS))
        finish(h)

    @pl.when(step != 0)
    def _():
        h = h_ref[...]
        n = _rms_norm(h, ng_ref[...]).astype(BF16)

        def proj(base, g):
            return _dot(n, win_s[:, cols(base, g)]) + bin_ref[:, cols(base, g)]

        vs = []
        for g in groups:
            vs.append(_gelu(proj(COL_V, g)))
            gate_b = proj(COL_B, g)
            xg = proj(COL_C, g) * proj(COL_X, g)
            conv_branch(g, gate_b, xg, functools.partial(proj, COL_GB, g))
        norm_v(vs)
        for g in groups:
            mix_branch(g, _gelu(proj(COL_U, g)), functools.partial(proj, COL_GA, g))
        finish(h)

    @pl.when(step == pl.num_programs(0) - 1)
    def _():
        hs = hs_ref[...]
        ns = _rms_norm(hs, ng_ref[...]).astype(BF16)

        def proj_s(col, width=COL_BLOCK):
            return _dot(ns, win_s[:, col:col + width]) + bin_ref[:, col:col + width]

        v = _layer_norm(_gelu(proj_s(COL_V, E_A)), lng_ref[...], lnb_ref[...])
        vs_ref[...] = v
        for g in range(G_A):
            c0 = g * COL_BLOCK
            cs = slice(c0, c0 + COL_BLOCK)
            u = _gelu(proj_s(COL_U + c0))
            z = ws_ref[g][0:1, 0:1] * v[:, cs] + bs_ref[g][0:1, :]
            y_a = u * z
            gate_b = proj_s(COL_B + c0)
            xg = proj_s(COL_C + c0) * proj_s(COL_X + c0)
            older, newer = st_ref[:, cs], st_ref[:, E_B + c0:E_B + c0 + COL_BLOCK]
            sts_ref[:, cs] = newer
            sts_ref[:, E_B + c0:E_B + c0 + COL_BLOCK] = xg
            conv = cw_ref[0:1, cs] * older
            conv = conv + cw_ref[1:2, cs] * newer
            conv = conv + cw_ref[2:3, cs] * xg
            y_b = gate_b * conv
            m = jax.nn.sigmoid(proj_s(COL_GA + c0)) * y_a + jax.nn.sigmoid(proj_s(COL_GB + c0)) * y_b
            ms_s[:, cs] = m.astype(BF16)
        os_ref[...] = hs + _dot(ms_s[...], wout_s[...])


def _mix(h, hs, state, norm_g, w_in, b_in, ln_g, ln_b, w_s, b_s, conv_w, w_out, *, layer, seq):
    rows, srows = h.shape[0], hs.shape[0]
    tm = ROW_TILE
    tiles_per_seq = seq // tm
    assert rows // tm > 1
    row_spec = pl.BlockSpec((tm, D_MODEL), lambda i: (i, 0))
    sample = _resident((srows, D_MODEL))
    sample_out = _whole_out((srows, D_MODEL))
    return pl.pallas_call(
        functools.partial(_mix_kernel, layer=layer, tm=tm, tiles_per_seq=tiles_per_seq),
        grid=(rows // tm,),
        in_specs=[row_spec, sample, _layer_resident((srows, (CONV_W - 1) * E_B), layer),
                  _layer_resident((1, D_MODEL), layer), _HBM, _layer_resident((1, IN_COLS), layer),
                  _layer_resident((1, E_A), layer), _layer_resident((1, E_A), layer),
                  _layer_resident((G_A, CHUNK, CHUNK), layer), _resident((G_A, CHUNK, 1)),
                  _layer_resident((CONV_W, E_B), layer), _HBM],
        out_specs=[row_spec, pl.BlockSpec((1, CONV_W - 1, E_B), lambda i: (i // tiles_per_seq, 0, 0)),
                   sample_out, _whole_out((srows, (CONV_W - 1) * E_B)), sample_out],
        out_shape=[jax.ShapeDtypeStruct((rows, D_MODEL), F32),
                   jax.ShapeDtypeStruct((rows // seq, CONV_W - 1, E_B), F32),
                   jax.ShapeDtypeStruct((srows, D_MODEL), F32),
                   jax.ShapeDtypeStruct((srows, (CONV_W - 1) * E_B), F32),
                   jax.ShapeDtypeStruct((srows, E_A), F32)],
        scratch_shapes=[pltpu.VMEM((D_MODEL, IN_COLS), BF16), pltpu.VMEM((D_MODEL, D_MODEL), BF16),
                        pltpu.VMEM((MIX_SLOTS, D_MODEL, D_MODEL), F32), pltpu.SemaphoreType.DMA((MIX_SLOTS,)),
                        pltpu.VMEM((tm, E_A), BF16), pltpu.VMEM((HALO + tm, E_B), F32),
                        pltpu.VMEM((tm, E_B), F32), pltpu.VMEM((tm, D_MODEL), BF16),
                        pltpu.VMEM((srows, D_MODEL), BF16)],
        compiler_params=_params(),
        name="mix",
    )(h, hs, state, norm_g, w_in, b_in, ln_g, ln_b, w_s, b_s, conv_w, w_out)


def kernel(x_prompt, x_sample, state_conv, ffn1_norm, ffn1_w_gate, ffn1_w_up, ffn1_w_down, mix_norm, w_in, b_in,
           v_ln_gain, v_ln_bias, w_spatial, b_spatial, conv_w, w_out, ffn2_norm, ffn2_w_gate, ffn2_w_up,
           ffn2_w_down, final_norm):
    batch, seq, _ = x_prompt.shape
    dec_batch, dec_seq, _ = x_sample.shape
    depth = w_in.shape[0]
    assert dec_seq == 1 and seq % ROW_TILE == 0 and ROW_TILE % CHUNK == 0

    hp = x_prompt.reshape(batch * seq, D_MODEL)
    hs = x_sample.reshape(dec_batch, D_MODEL)
    fin_g = final_norm.reshape(1, D_MODEL)
    rows_of = lambda a: a.reshape(depth, 1, -1)
    state = state_conv.reshape(depth, dec_batch, (CONV_W - 1) * E_B)
    conv_p, conv_s, v_s = [], [], []
    for l in range(depth):
        last = l == depth - 1
        hp, hs = _ffn(hp, hs, rows_of(ffn1_norm), ffn1_w_gate, ffn1_w_up, ffn1_w_down, fin_g, layer=l, final=False)
        hp, buf_p, hs, state_s, vn_s = _mix(
            hp, hs, state, rows_of(mix_norm), w_in, rows_of(b_in), rows_of(v_ln_gain), rows_of(v_ln_bias),
            w_spatial, b_spatial[l][:, :, None], conv_w, w_out, layer=l, seq=seq)
        hp, hs = _ffn(hp, hs, rows_of(ffn2_norm), ffn2_w_gate, ffn2_w_up, ffn2_w_down, fin_g, layer=l, final=last)

        conv_p.append(buf_p)
        conv_s.append(state_s.reshape(dec_batch, CONV_W - 1, E_B))
        v_s.append(vn_s.reshape(dec_batch, dec_seq, E_A))

    return (hp.reshape(batch, seq, D_MODEL), hs.reshape(dec_batch, dec_seq, D_MODEL),
            jnp.stack(conv_p, axis=0), jnp.stack(conv_s, axis=0), jnp.stack(v_s, axis=0))
```

```python
import functools

import jax
import jax.numpy as jnp
import numpy as np
from jax import lax
from jax.experimental import pallas as pl
from jax.experimental.pallas import tpu as pltpu

D_MODEL = 1024
E_A = D_MODEL
G_A = 4
HD_A = E_A // G_A
CHUNK = 128
E_B = D_MODEL
CONV_W = 3
D_FF = 2816
EPS = 1e-6
SQRT_HALF = float(np.sqrt(0.5).astype(np.float32))
IN_COLS = 2 * E_A + 3 * E_B + 2 * D_MODEL
COL_U, COL_V, COL_B, COL_C, COL_X, COL_GA, COL_GB = (k * D_MODEL for k in range(7))

ROW_TILE = 512
FF_CHUNK = 256
COL_BLOCK = HD_A
HALO = 8
FFN_SLOTS = 3
MIX_SLOTS = 3
ROWS_W_IN = 128
ROWS_W_OUT = 256
DMA_PRIORITIES = 2
VMEM_LIMIT_BYTES = 56 * 1024 * 1024

F32 = jnp.float32
BF16 = jnp.bfloat16


def _rms_norm(x, g):
    return (x * lax.rsqrt(jnp.mean(x * x, axis=-1, keepdims=True) + EPS)) * g


def _layer_norm(x, g, b):
    mu = jnp.mean(x, axis=-1, keepdims=True)
    xc = x - mu
    return (xc * lax.rsqrt(jnp.mean(xc * xc, axis=-1, keepdims=True) + EPS)) * g + b


def _gelu(x):
    return 0.5 * x * (1.0 + lax.erf(x * SQRT_HALF))


def _dot(a, b):
    return jnp.dot(a, b, preferred_element_type=F32)


def _resident(shape):
    return pl.BlockSpec(shape, lambda i: (0,) * len(shape), pipeline_mode=pl.Buffered(1))


def _layer_resident(shape, layer):
    return pl.BlockSpec((None,) + shape, lambda i: (layer,) + (0,) * len(shape), pipeline_mode=pl.Buffered(1))


def _whole_out(shape):
    return pl.BlockSpec(shape, lambda i: (0,) * len(shape))


_HBM = pl.BlockSpec(memory_space=pl.ANY)


def _params():
    return pltpu.CompilerParams(dimension_semantics=("arbitrary",), vmem_limit_bytes=VMEM_LIMIT_BYTES)


def _no_weights_needed(chunk):
    del chunk


def _stream_cast(src, dst, stage, sem):
    slots, rows = stage.shape[0], stage.shape[1]
    n = src.shape[0] // rows
    ahead = slots - 1
    assert n * rows == src.shape[0] and n >= ahead >= 1

    def copy(k):
        return pltpu.make_async_copy(src.at[pl.ds(k * rows, rows), :], stage.at[k % slots], sem.at[k % slots])

    for k in range(ahead):
        copy(k).start(priority=k % DMA_PRIORITIES)
    for k in range(n):
        copy(k).wait()
        if k + ahead < n:
            copy(k + ahead).start(priority=(k + ahead) % DMA_PRIORITIES)
        dst[k * rows:(k + 1) * rows, :] = stage[k % slots].astype(BF16)


def _ffn_kernel(x_ref, xs_ref, g_ref, wg_hbm, wu_hbm, wd_hbm, fg_ref, o_ref, os_ref,
                wg_s, wu_s, wd_s, stage_in, stage_out, sem, *, layer, final):
    step = pl.program_id(0)
    n_chunks = D_FF // FF_CHUNK

    def copies(c):
        slot = c % FFN_SLOTS
        cols = pl.ds(c * FF_CHUNK, FF_CHUNK)
        return (pltpu.make_async_copy(wg_hbm.at[layer, :, cols], stage_in.at[slot, 0], sem.at[slot, 0]),
                pltpu.make_async_copy(wu_hbm.at[layer, :, cols], stage_in.at[slot, 1], sem.at[slot, 1]),
                pltpu.make_async_copy(wd_hbm.at[layer, cols, :], stage_out.at[slot], sem.at[slot, 2]))

    def fetch(c):
        for j, cp in enumerate(copies(c)):
            cp.start(priority=(c + j) % DMA_PRIORITIES)

    def land(c):
        slot = c % FFN_SLOTS
        sl = slice(c * FF_CHUNK, (c + 1) * FF_CHUNK)
        for cp in copies(c):
            cp.wait()
        wg_s[:, sl] = stage_in[slot, 0].astype(BF16)
        wu_s[:, sl] = stage_in[slot, 1].astype(BF16)
        wd_s[sl, :] = stage_out[slot].astype(BF16)
        if c + FFN_SLOTS < n_chunks:
            fetch(c + FFN_SLOTS)

    def half_step(x, need):
        xn = _rms_norm(x, g_ref[...]).astype(BF16)
        acc = None
        for c in range(n_chunks):
            need(c)
            sl = slice(c * FF_CHUNK, (c + 1) * FF_CHUNK)
            gate = _dot(xn, wg_s[:, sl])
            up = _dot(xn, wu_s[:, sl])
            a = (jax.nn.silu(gate) * up).astype(BF16)
            d = _dot(a, wd_s[sl, :])
            acc = d if acc is None else acc + d
        h = x + 0.5 * acc
        if final:
            h = _rms_norm(h, fg_ref[...])
        return h

    @pl.when(step == 0)
    def _():
        for c in range(FFN_SLOTS):
            fetch(c)
        o_ref[...] = half_step(x_ref[...], land)

    @pl.when(step != 0)
    def _():
        o_ref[...] = half_step(x_ref[...], _no_weights_needed)

    @pl.when(step == pl.num_programs(0) - 1)
    def _():
        os_ref[...] = half_step(xs_ref[...], _no_weights_needed)


def _ffn(x, xs, norm_g, wg, wu, wd, final_g, *, layer, final):
    rows, srows = x.shape[0], xs.shape[0]
    assert rows // ROW_TILE > 1
    row_spec = pl.BlockSpec((ROW_TILE, D_MODEL), lambda i: (i, 0))
    return pl.pallas_call(
        functools.partial(_ffn_kernel, layer=layer, final=final),
        grid=(rows // ROW_TILE,),
        in_specs=[row_spec, _resident((srows, D_MODEL)), _layer_resident((1, D_MODEL), layer), _HBM, _HBM, _HBM,
                  _resident((1, D_MODEL))],
        out_specs=[row_spec, _whole_out((srows, D_MODEL))],
        out_shape=[jax.ShapeDtypeStruct((rows, D_MODEL), F32), jax.ShapeDtypeStruct((srows, D_MODEL), F32)],
        scratch_shapes=[pltpu.VMEM((D_MODEL, D_FF), BF16), pltpu.VMEM((D_MODEL, D_FF), BF16),
                        pltpu.VMEM((D_FF, D_MODEL), BF16),
                        pltpu.VMEM((FFN_SLOTS, 2, D_MODEL, FF_CHUNK), F32),
                        pltpu.VMEM((FFN_SLOTS, FF_CHUNK, D_MODEL), F32),
                        pltpu.SemaphoreType.DMA((FFN_SLOTS, 3))],
        compiler_params=_params(),
        name="ffn_final" if final else "ffn",
    )(x, xs, norm_g, wg, wu, wd, final_g)


def _causal_spatial(ws_ref, g):
    t = lax.broadcasted_iota(jnp.int32, (CHUNK, CHUNK), 0)
    s = lax.broadcasted_iota(jnp.int32, (CHUNK, CHUNK), 1)
    return jnp.where(s <= t, ws_ref[g], 0.0).astype(BF16)


def _mix_kernel(h_ref, hs_ref, st_ref, ng_ref, win_hbm, bin_ref, lng_ref, lnb_ref, ws_ref, bs_ref, cw_ref,
                wout_hbm, o_ref, buf_ref, os_ref, sts_ref, vs_ref,
                win_s, wout_s, stage_in, stage_out, sem, vn_s, xg_s, yb_s, m_s, ms_s, *, layer, tm, tiles_per_seq):
    step = pl.program_id(0)

    @pl.when(step == 0)
    def _():
        _stream_cast(win_hbm.at[layer], win_s, stage_in, sem)
        _stream_cast(wout_hbm.at[layer], wout_s, stage_out, sem)

    @pl.when(step % tiles_per_seq == 0)
    def _():
        xg_s[0:HALO, :] = jnp.zeros((HALO, E_B), F32)

    h = h_ref[...]
    n = _rms_norm(h, ng_ref[...]).astype(BF16)

    def proj(col, width=COL_BLOCK):
        return _dot(n, win_s[:, col:col + width]) + bin_ref[:, col:col + width]

    vs = []
    for g in range(G_A):
        c0 = g * COL_BLOCK
        cs = slice(c0, c0 + COL_BLOCK)
        vs.append(_gelu(proj(COL_V + c0)))
        gate_b = proj(COL_B + c0)
        xg = proj(COL_C + c0) * proj(COL_X + c0)
        xg_s[HALO:HALO + tm, cs] = xg
        conv = cw_ref[0:1, cs] * xg_s[HALO - 2:HALO - 2 + tm, cs]
        conv = conv + cw_ref[1:2, cs] * xg_s[HALO - 1:HALO - 1 + tm, cs]
        conv = conv + cw_ref[2:3, cs] * xg
        y_b = gate_b * conv
        yb_s[:, cs] = jax.nn.sigmoid(proj(COL_GB + c0)) * y_b

    mu = sum(jnp.sum(v, axis=-1, keepdims=True) for v in vs) / E_A
    vs = [v - mu for v in vs]
    var = sum(jnp.sum(v * v, axis=-1, keepdims=True) for v in vs) / E_A
    rstd = lax.rsqrt(var + EPS)
    for g in range(G_A):
        cs = slice(g * COL_BLOCK, (g + 1) * COL_BLOCK)
        vn_s[:, cs] = ((vs[g] * rstd) * lng_ref[:, cs] + lnb_ref[:, cs]).astype(BF16)

    for g in range(G_A):
        c0 = g * COL_BLOCK
        cs = slice(c0, c0 + COL_BLOCK)
        u = _gelu(proj(COL_U + c0))
        w_mix = _causal_spatial(ws_ref, g)
        z = jnp.concatenate(
            [_dot(w_mix, vn_s[c * CHUNK:(c + 1) * CHUNK, cs]) + bs_ref[g] for c in range(tm // CHUNK)], axis=0)
        y_a = u * z
        m = jax.nn.sigmoid(proj(COL_GA + c0)) * y_a + yb_s[:, cs]
        m_s[:, cs] = m.astype(BF16)

    last = xg_s[HALO + tm - 2:HALO + tm, :]
    buf_ref[0] = last
    xg_s[HALO - 2:HALO, :] = last
    o_ref[...] = h + _dot(m_s[...], wout_s[...])

    @pl.when(step == pl.num_programs(0) - 1)
    def _():
        hs = hs_ref[...]
        ns = _rms_norm(hs, ng_ref[...]).astype(BF16)

        def proj_s(col, width=COL_BLOCK):
            return _dot(ns, win_s[:, col:col + width]) + bin_ref[:, col:col + width]

        v = _layer_norm(_gelu(proj_s(COL_V, E_A)), lng_ref[...], lnb_ref[...])
        vs_ref[...] = v
        for g in range(G_A):
            c0 = g * COL_BLOCK
            cs = slice(c0, c0 + COL_BLOCK)
            u = _gelu(proj_s(COL_U + c0))
            z = ws_ref[g][0:1, 0:1] * v[:, cs] + bs_ref[g][0:1, :]
            y_a = u * z
            gate_b = proj_s(COL_B + c0)
            xg = proj_s(COL_C + c0) * proj_s(COL_X + c0)
            older, newer = st_ref[:, cs], st_ref[:, E_B + c0:E_B + c0 + COL_BLOCK]
            sts_ref[:, cs] = newer
            sts_ref[:, E_B + c0:E_B + c0 + COL_BLOCK] = xg
            conv = cw_ref[0:1, cs] * older
            conv = conv + cw_ref[1:2, cs] * newer
            conv = conv + cw_ref[2:3, cs] * xg
            y_b = gate_b * conv
            m = jax.nn.sigmoid(proj_s(COL_GA + c0)) * y_a + jax.nn.sigmoid(proj_s(COL_GB + c0)) * y_b
            ms_s[:, cs] = m.astype(BF16)
        os_ref[...] = hs + _dot(ms_s[...], wout_s[...])


def _mix(h, hs, state, norm_g, w_in, b_in, ln_g, ln_b, w_s, b_s, conv_w, w_out, *, layer, seq):
    rows, srows = h.shape[0], hs.shape[0]
    tm = ROW_TILE
    tiles_per_seq = seq // tm
    row_spec = pl.BlockSpec((tm, D_MODEL), lambda i: (i, 0))
    sample = _resident((srows, D_MODEL))
    sample_out = _whole_out((srows, D_MODEL))
    return pl.pallas_call(
        functools.partial(_mix_kernel, layer=layer, tm=tm, tiles_per_seq=tiles_per_seq),
        grid=(rows // tm,),
        in_specs=[row_spec, sample, _layer_resident((srows, (CONV_W - 1) * E_B), layer),
                  _layer_resident((1, D_MODEL), layer), _HBM, _layer_resident((1, IN_COLS), layer),
                  _layer_resident((1, E_A), layer), _layer_resident((1, E_A), layer),
                  _layer_resident((G_A, CHUNK, CHUNK), layer), _resident((G_A, CHUNK, 1)),
                  _layer_resident((CONV_W, E_B), layer), _HBM],
        out_specs=[row_spec, pl.BlockSpec((1, CONV_W - 1, E_B), lambda i: (i // tiles_per_seq, 0, 0)),
                   sample_out, _whole_out((srows, (CONV_W - 1) * E_B)), sample_out],
        out_shape=[jax.ShapeDtypeStruct((rows, D_MODEL), F32),
                   jax.ShapeDtypeStruct((rows // seq, CONV_W - 1, E_B), F32),
                   jax.ShapeDtypeStruct((srows, D_MODEL), F32),
                   jax.ShapeDtypeStruct((srows, (CONV_W - 1) * E_B), F32),
                   jax.ShapeDtypeStruct((srows, E_A), F32)],
        scratch_shapes=[pltpu.VMEM((D_MODEL, IN_COLS), BF16), pltpu.VMEM((D_MODEL, D_MODEL), BF16),
                        pltpu.VMEM((MIX_SLOTS, ROWS_W_IN, IN_COLS), F32),
                        pltpu.VMEM((MIX_SLOTS, ROWS_W_OUT, D_MODEL), F32),
                        pltpu.SemaphoreType.DMA((MIX_SLOTS,)),
                        pltpu.VMEM((tm, E_A), BF16), pltpu.VMEM((HALO + tm, E_B), F32),
                        pltpu.VMEM((tm, E_B), F32), pltpu.VMEM((tm, D_MODEL), BF16),
                        pltpu.VMEM((srows, D_MODEL), BF16)],
        compiler_params=_params(),
        name="mix",
    )(h, hs, state, norm_g, w_in, b_in, ln_g, ln_b, w_s, b_s, conv_w, w_out)


def kernel(x_prompt, x_sample, state_conv, ffn1_norm, ffn1_w_gate, ffn1_w_up, ffn1_w_down, mix_norm, w_in, b_in,
           v_ln_gain, v_ln_bias, w_spatial, b_spatial, conv_w, w_out, ffn2_norm, ffn2_w_gate, ffn2_w_up,
           ffn2_w_down, final_norm):
    batch, seq, _ = x_prompt.shape
    dec_batch, dec_seq, _ = x_sample.shape
    depth = w_in.shape[0]
    assert dec_seq == 1 and seq % ROW_TILE == 0 and ROW_TILE % CHUNK == 0

    hp = x_prompt.reshape(batch * seq, D_MODEL)
    hs = x_sample.reshape(dec_batch, D_MODEL)
    fin_g = final_norm.reshape(1, D_MODEL)
    rows_of = lambda a: a.reshape(depth, 1, -1)
    state = state_conv.reshape(depth, dec_batch, (CONV_W - 1) * E_B)
    conv_p, conv_s, v_s = [], [], []
    for l in range(depth):
        last = l == depth - 1
        hp, hs = _ffn(hp, hs, rows_of(ffn1_norm), ffn1_w_gate, ffn1_w_up, ffn1_w_down, fin_g, layer=l, final=False)
        hp, buf_p, hs, state_s, vn_s = _mix(
            hp, hs, state, rows_of(mix_norm), w_in, rows_of(b_in), rows_of(v_ln_gain), rows_of(v_ln_bias),
            w_spatial, b_spatial[l][:, :, None], conv_w, w_out, layer=l, seq=seq)
        hp, hs = _ffn(hp, hs, rows_of(ffn2_norm), ffn2_w_gate, ffn2_w_up, ffn2_w_down, fin_g, layer=l, final=last)

        conv_p.append(buf_p)
        conv_s.append(state_s.reshape(dec_batch, CONV_W - 1, E_B))
        v_s.append(vn_s.reshape(dec_batch, dec_seq, E_A))

    return (hp.reshape(batch, seq, D_MODEL), hs.reshape(dec_batch, dec_seq, D_MODEL),
            jnp.stack(conv_p, axis=0), jnp.stack(conv_s, axis=0), jnp.stack(v_s, axis=0))
```
